```python
import jax
import jax.numpy as jnp
from jax import lax
import numpy as np

D_MODEL = 1024
BATCH = 4
SEQ = 8192
DEPTH = 2

GRID_W = 64
CTX_LEN = 256

RET_HEADS = 4
RET_DK = 128
RET_DV = 256
RET_CHUNK = 128
RET_GAMMA_EXP0 = 5

NA_HEADS = 8
NA_DH = 64
NA_KH_MAX = 8
NA_KW = 16

FOU_GROUPS = 4
FOU_DG = 128

N_BRANCH = 3
D_FF = 2816
ROPE_BASE = 10000.0
LN_EPS = 1e-6

RET_QK_W = RET_HEADS * RET_DK
RET_V_W = RET_HEADS * RET_DV
NA_W = NA_HEADS * NA_DH
FOU_W = FOU_GROUPS * FOU_DG
SPLIT_SIZES = (RET_QK_W, RET_QK_W, RET_V_W, RET_V_W, NA_W, NA_W, NA_W, FOU_W, N_BRANCH * D_MODEL)
SPLIT_POINTS = tuple(sum(SPLIT_SIZES[:i + 1]) for i in range(len(SPLIT_SIZES) - 1))
D_IN = sum(SPLIT_SIZES)
DEEPNORM_ALPHA = (2 * DEPTH) ** 0.25
DEEPNORM_BETA = (8 * DEPTH) ** -0.25

kernel_name = 'hybrid_retention_natten_fnet_prefix_trunk'


def layer_norm(x, w=None, b=None):
    xf = x.astype(jnp.float32)
    mu = xf.mean(-1, keepdims=True)
    var = jnp.square(xf - mu).mean(-1, keepdims=True)
    y = (xf - mu) * lax.rsqrt(var + LN_EPS)
    if w is not None:
        y = y * w.astype(jnp.float32) + b.astype(jnp.float32)
    return y.astype(x.dtype)


def modulate(x, shift, scale):
    return layer_norm(x) * (1.0 + scale) + shift


def swiglu(h, w1, w3, w2):
    return (jax.nn.silu(h @ w1) * (h @ w3)) @ w2


def ffn_sublayer(s, mods, w1, w3, w2, lw, lb):
    shift, scale, gate = mods
    h = modulate(s, shift, scale)
    return layer_norm(DEEPNORM_ALPHA * s + 0.5 * gate * swiglu(h, w1, w3, w2), lw, lb)


def to_heads(t, n_heads, d):
    b, n, _ = t.shape
    return t.reshape(b, n, n_heads, d).transpose(0, 2, 1, 3)


def axial_rope(x, row, col):
    def rot(xa, pos):
        half = xa.shape[-1] // 2
        freqs = ROPE_BASE ** (-jnp.arange(half, dtype=jnp.float32) / half)
        ang = pos.astype(jnp.float32)[:, None] * freqs[None, :]
        cos = jnp.cos(ang).astype(xa.dtype)
        sin = jnp.sin(ang).astype(xa.dtype)
        x1, x2 = xa[..., :half], xa[..., half:]
        return jnp.concatenate([x1 * cos - x2 * sin, x1 * sin + x2 * cos], axis=-1)
    h = x.shape[-1] // 2
    return jnp.concatenate([rot(x[..., :h], row), rot(x[..., h:], col)], axis=-1)


def retention_scan(q, k, v, log_g, s0, include_diag):
    b, h, n, dk = q.shape
    dv = v.shape[-1]
    nc = n // RET_CHUNK
    idx = jnp.arange(RET_CHUNK, dtype=jnp.float32)
    diff = idx[:, None] - idx[None, :]
    mask = diff >= 0 if include_diag else diff > 0
    dmat = jnp.where(mask[None], jnp.exp(log_g[:, None, None] * jnp.maximum(diff, 0.0)[None]), 0.0)
    q_decay = jnp.exp(log_g[:, None] * (idx + 1.0)[None, :])
    k_decay = jnp.exp(log_g[:, None] * (RET_CHUNK - 1.0 - idx)[None, :])
    chunk_decay = jnp.exp(log_g * RET_CHUNK)

    def chunks(t):
        return t.reshape(b, h, nc, RET_CHUNK, t.shape[-1]).transpose(2, 0, 1, 3, 4)

    def step(s, inp):
        qi, ki, vi = inp
        inner = jnp.einsum('bhid,bhjd->bhij', qi, ki) * dmat
        o = jnp.einsum('bhij,bhjv->bhiv', inner, vi) + jnp.einsum('bhid,bhdv->bhiv', qi, s) * q_decay[:, :, None]
        s = s * chunk_decay[:, None, None] + jnp.einsum('bhjd,bhjv->bhdv', ki * k_decay[:, :, None], vi)
        return s, o

    _, o = lax.scan(step, s0, (chunks(q), chunks(k), chunks(v)))
    return o.transpose(1, 2, 0, 3, 4).reshape(b, h, n, dv)


def retention_final_state(k, v, log_g, reverse):
    n = k.shape[2]
    m = jnp.arange(n, dtype=jnp.float32)
    e = m if reverse else (n - 1.0 - m)
    w = jnp.exp(log_g[:, None] * e[None, :])
    return jnp.einsum('bhmd,bhmv->bhdv', k * w[None, :, :, None], v)


def bidir_retention(q, k, v, log_g, s_fwd, s_bwd):
    flip = lambda a: jnp.flip(a, axis=2)
    fwd = retention_scan(q, k, v, log_g[0], s_fwd, True)
    bwd = flip(retention_scan(flip(q), flip(k), flip(v), log_g[1], s_bwd, False))
    return fwd + bwd


def retention_out(o, g, gn_w, gn_b, w_out):
    mu = o.mean(-1, keepdims=True)
    var = jnp.square(o - mu).mean(-1, keepdims=True)
    o = (o - mu) * lax.rsqrt(var + LN_EPS)
    b, h, n, dv = o.shape
    o = o.transpose(0, 2, 1, 3).reshape(b, n, h * dv) * gn_w.astype(jnp.float32) + gn_b.astype(jnp.float32)
    return (o.astype(g.dtype) * jax.nn.silu(g)) @ w_out


def neighbourhood_attention(q, k, v, kc, vc, rpb, rows):
    kh = min(NA_KH_MAX, rows)
    scale = NA_DH ** -0.5
    cols = jnp.arange(GRID_W)
    col_start = jnp.clip(cols - NA_KW // 2, 0, GRID_W - NA_KW)
    col_idx = col_start[:, None] + jnp.arange(NA_KW)[None, :]
    dc = col_idx - cols[:, None] + NA_KW - 1
    rpb_c = rpb[:, :, dc]

    def row_fn(r):
        r0 = jnp.clip(r - kh // 2, 0, rows - kh)
        qr = lax.dynamic_index_in_dim(q, r, axis=2, keepdims=False)
        kb = lax.dynamic_slice_in_dim(k, r0, kh, axis=2)
        vb = lax.dynamic_slice_in_dim(v, r0, kh, axis=2)
        kn = kb[:, :, :, col_idx]
        vn = vb[:, :, :, col_idx]
        dr = r0 + jnp.arange(kh) - r + NA_KH_MAX - 1
        bias = jnp.transpose(rpb_c[:, dr], (0, 2, 1, 3))
        s_loc = jnp.einsum('bhqd,bhiqjd->bhqij', qr, kn) * scale + bias
        s_ctx = jnp.einsum('bhqd,bhmd->bhqm', qr, kc) * scale
        bsz, hh, wq = s_loc.shape[:3]
        s = jnp.concatenate([s_loc.reshape(bsz, hh, wq, kh * NA_KW), s_ctx], axis=-1).astype(jnp.float32)
        p = jax.nn.softmax(s, axis=-1).astype(v.dtype)
        p_loc = p[..., :kh * NA_KW].reshape(bsz, hh, wq, kh, NA_KW)
        p_ctx = p[..., kh * NA_KW:]
        return jnp.einsum('bhqij,bhiqjd->bhqd', p_loc, vn) + jnp.einsum('bhqm,bhmd->bhqd', p_ctx, vc)

    return lax.map(row_fn, jnp.arange(rows))


def context_attention(q, k, v):
    s = (jnp.einsum('bhnd,bhmd->bhnm', q, k) * NA_DH ** -0.5).astype(jnp.float32)
    p = jax.nn.softmax(s, axis=-1).astype(v.dtype)
    return jnp.einsum('bhnm,bhmd->bhnd', p, v)


def fourier_mix(u):
    b, n, _ = u.shape
    ug = u.reshape(b, n, FOU_GROUPS, FOU_DG).astype(jnp.float32)
    f = jnp.fft.fft2(ug, axes=(1, 3), norm='ortho').real
    return f.reshape(b, n, FOU_W).astype(u.dtype)


def token_mixer(hx, hc, w_in, ret_decay_logit, ret_gn_w, ret_gn_b, ret_w_out, na_rpb, na_w_out,
                fou_w_out, w_o, ctx_out):
    b, n_lat, _ = hx.shape
    rows = n_lat // GRID_W
    rq_x, rk_x, rv_x, rg_x, nq_x, nk_x, nv_x, fu_x, gt_x = jnp.split(hx @ w_in, SPLIT_POINTS, axis=-1)
    rq_c, rk_c, rv_c, rg_c, nq_c, nk_c, nv_c, fu_c, gt_c = jnp.split(hc @ w_in, SPLIT_POINTS, axis=-1)
    log_g = jax.nn.log_sigmoid(ret_decay_logit.astype(jnp.float32))
    f32 = jnp.float32

    t = jnp.arange(n_lat)
    row, col = t // GRID_W, t % GRID_W
    q_x = axial_rope(to_heads(rq_x, RET_HEADS, RET_DK).astype(f32) * RET_DK ** -0.5, row, col)
    k_x = axial_rope(to_heads(rk_x, RET_HEADS, RET_DK).astype(f32), row, col)
    v_x = to_heads(rv_x, RET_HEADS, RET_DV).astype(f32)
    k_c = to_heads(rk_c, RET_HEADS, RET_DK).astype(f32)
    v_c = to_heads(rv_c, RET_HEADS, RET_DV).astype(f32)
    s_fwd = retention_final_state(k_c, v_c, log_g[0], False)
    s_bwd = retention_final_state(k_c, v_c, log_g[1], True)
    y_ret = retention_out(bidir_retention(q_x, k_x, v_x, log_g, s_fwd, s_bwd), rg_x, ret_gn_w, ret_gn_b, ret_w_out)

    def grid_heads(a):
        return a.reshape(b, rows, GRID_W, NA_HEADS, NA_DH).transpose(0, 3, 1, 2, 4)
    kc_na = to_heads(nk_c, NA_HEADS, NA_DH)
    vc_na = to_heads(nv_c, NA_HEADS, NA_DH)
    o_na = neighbourhood_attention(grid_heads(nq_x), grid_heads(nk_x), grid_heads(nv_x), kc_na, vc_na, na_rpb, rows)
    y_na = o_na.transpose(1, 0, 3, 2, 4).reshape(b, n_lat, NA_W) @ na_w_out

    y_fou = fourier_mix(fu_x) @ fou_w_out

    g_ret, g_na, g_fou = jnp.split(gt_x, N_BRANCH, axis=-1)
    y_x = (jax.nn.sigmoid(g_ret) * y_ret + jax.nn.sigmoid(g_na) * y_na + jax.nn.sigmoid(g_fou) * y_fou) @ w_o
    if not ctx_out:
        return y_x, None

    n_ctx = hc.shape[1]
    q_c = to_heads(rq_c, RET_HEADS, RET_DK).astype(f32) * RET_DK ** -0.5
    zeros = jnp.zeros((b, RET_HEADS, RET_DK, RET_DV), f32)
    yc_ret = retention_out(bidir_retention(q_c, k_c, v_c, log_g, zeros, zeros), rg_c, ret_gn_w, ret_gn_b, ret_w_out)
    o_cna = context_attention(to_heads(nq_c, NA_HEADS, NA_DH), kc_na, vc_na)
    yc_na = o_cna.transpose(0, 2, 1, 3).reshape(b, n_ctx, NA_W) @ na_w_out
    yc_fou = fourier_mix(fu_c) @ fou_w_out
    gc_ret, gc_na, gc_fou = jnp.split(gt_c, N_BRANCH, axis=-1)
    y_c = (jax.nn.sigmoid(gc_ret) * yc_ret + jax.nn.sigmoid(gc_na) * yc_na + jax.nn.sigmoid(gc_fou) * yc_fou) @ w_o
    return y_x, y_c


def setup_inputs(seed: int = 0) -> dict:
    key = jax.random.key(seed)
    ks = jax.random.split(key, 26)
    f32 = jnp.float32
    beta = DEEPNORM_BETA

    def nrm(k, shape, scale):
        return jax.random.normal(k, shape, f32) * scale

    col_scales = (1.0, 1.0, beta, 1.0, 1.0, 1.0, beta, beta, 1.0)
    col_scale = jnp.concatenate([jnp.full((s,), v, f32) for s, v in zip(SPLIT_SIZES, col_scales)])
    gamma0 = 1.0 - 2.0 ** (-(RET_GAMMA_EXP0 + np.arange(RET_HEADS)))
    logit0 = jnp.asarray(np.log(gamma0 / (1.0 - gamma0)), f32)
    return {
        'x': nrm(ks[0], (BATCH, SEQ, D_MODEL), 1.0),
        'c': nrm(ks[1], (BATCH, D_MODEL), 1.0),
        'ctx': nrm(ks[2], (BATCH, CTX_LEN, D_MODEL), 1.0),
        'c_ctx': nrm(ks[3], (D_MODEL,), 1.0),
        'ada_w': nrm(ks[4], (DEPTH, D_MODEL, 9 * D_MODEL), 0.5 * D_MODEL ** -0.5),
        'ada_b': nrm(ks[5], (DEPTH, 9 * D_MODEL), 0.02),
        'ln_w': 1.0 + nrm(ks[6], (DEPTH, 3, D_MODEL), 0.02),
        'ln_b': nrm(ks[7], (DEPTH, 3, D_MODEL), 0.02),
        'ffa_w1': nrm(ks[8], (DEPTH, D_MODEL, D_FF), D_MODEL ** -0.5),
        'ffa_w3': nrm(ks[9], (DEPTH, D_MODEL, D_FF), D_MODEL ** -0.5),
        'ffa_w2': nrm(ks[10], (DEPTH, D_FF, D_MODEL), beta * D_FF ** -0.5),
        'mix_w_in': nrm(ks[11], (DEPTH, D_MODEL, D_IN), D_MODEL ** -0.5) * col_scale,
        'ret_decay_logit': logit0[None, None, :] + nrm(ks[12], (DEPTH, 2, RET_HEADS), 0.1),
        'ret_gn_w': 1.0 + nrm(ks[13], (DEPTH, RET_V_W), 0.02),
        'ret_gn_b': nrm(ks[14], (DEPTH, RET_V_W), 0.02),
        'ret_w_out': nrm(ks[15], (DEPTH, RET_V_W, D_MODEL), beta * RET_V_W ** -0.5),
        'na_rpb': nrm(ks[16], (DEPTH, NA_HEADS, 2 * NA_KH_MAX - 1, 2 * NA_KW - 1), 0.1),
        'na_w_out': nrm(ks[17], (DEPTH, NA_W, D_MODEL), beta * NA_W ** -0.5),
        'fou_w_out': nrm(ks[18], (DEPTH, FOU_W, D_MODEL), beta * FOU_W ** -0.5),
        'mix_w_o': nrm(ks[19], (DEPTH, D_MODEL, D_MODEL), beta * D_MODEL ** -0.5),
        'ffb_w1': nrm(ks[20], (DEPTH, D_MODEL, D_FF), D_MODEL ** -0.5),
        'ffb_w3': nrm(ks[21], (DEPTH, D_MODEL, D_FF), D_MODEL ** -0.5),
        'ffb_w2': nrm(ks[22], (DEPTH, D_FF, D_MODEL), beta * D_FF ** -0.5),
    }


def reference(x, c, ctx, c_ctx, ada_w, ada_b, ln_w, ln_b, ffa_w1, ffa_w3, ffa_w2, mix_w_in,
              ret_decay_logit, ret_gn_w, ret_gn_b, ret_w_out, na_rpb, na_w_out, fou_w_out, mix_w_o,
              ffb_w1, ffb_w3, ffb_w2):
    for l in range(DEPTH):
        last = l == DEPTH - 1
        mods_x = jnp.split((jax.nn.silu(c) @ ada_w[l] + ada_b[l])[:, None, :], 9, axis=-1)
        mods_c = jnp.split((jax.nn.silu(c_ctx) @ ada_w[l] + ada_b[l])[None, None, :], 9, axis=-1)

        x = ffn_sublayer(x, mods_x[0:3], ffa_w1[l], ffa_w3[l], ffa_w2[l], ln_w[l, 0], ln_b[l, 0])
        ctx = ffn_sublayer(ctx, mods_c[0:3], ffa_w1[l], ffa_w3[l], ffa_w2[l], ln_w[l, 0], ln_b[l, 0])

        sh_x, sc_x, g_x = mods_x[3:6]
        sh_c, sc_c, g_c = mods_c[3:6]
        y_x, y_c = token_mixer(modulate(x, sh_x, sc_x), modulate(ctx, sh_c, sc_c), mix_w_in[l],
                               ret_decay_logit[l], ret_gn_w[l], ret_gn_b[l], ret_w_out[l], na_rpb[l],
                               na_w_out[l], fou_w_out[l], mix_w_o[l], not last)
        x = layer_norm(DEEPNORM_ALPHA * x + g_x * y_x, ln_w[l, 1], ln_b[l, 1])

        x = ffn_sublayer(x, mods_x[6:9], ffb_w1[l], ffb_w3[l], ffb_w2[l], ln_w[l, 2], ln_b[l, 2])
        if not last:
            ctx = layer_norm(DEEPNORM_ALPHA * ctx + g_c * y_c, ln_w[l, 1], ln_b[l, 1])
            ctx = ffn_sublayer(ctx, mods_c[6:9], ffb_w1[l], ffb_w3[l], ffb_w2[l], ln_w[l, 2], ln_b[l, 2])
    return x
```

```python
import functools
import math

import numpy as np
import jax
import jax.numpy as jnp
from jax import lax
from jax.experimental import pallas as pl
from jax.experimental.pallas import tpu as pltpu

F32 = jnp.float32
BF16 = jnp.bfloat16

GRID_W = 64
RET_HEADS = 4
RET_DK = 128
RET_DV = 256
NA_HEADS = 8
NA_DH = 64
NA_KH = 8
NA_KW = 16
FOU_GROUPS = 4
FOU_DG = 128
N_BRANCH = 3
ROPE_BASE = 10000.0
LN_EPS = 1e-6

V7X_VMEM_BYTES = 64 * 1024 * 1024
V7X_LANES = 128
V7X_SUBLANES = 8

TOKEN_TILE = 512
RET_CHUNK = 256
NA_GROUP_ROWS = 4
NA_KEY_ROWS = NA_GROUP_ROWS + NA_KH
FOU_K1_BLOCK = V7X_SUBLANES
NEG_BIG = -1e30


def _cparams(sem, vmem_mb):
    return pltpu.CompilerParams(dimension_semantics=sem,
                                vmem_limit_bytes=int(vmem_mb * 1024 * 1024))


def _resident(shape):
    nd = len(shape)
    return pl.BlockSpec(shape, lambda *_: (0,) * nd, pipeline_mode=pl.Buffered(1))


def _ln_rows(x):
    mu = jnp.mean(x, axis=-1, keepdims=True)
    xc = x - mu
    var = jnp.mean(xc * xc, axis=-1, keepdims=True)
    return xc * lax.rsqrt(var + LN_EPS)


def _sigmoid(x):
    return 1.0 / (1.0 + jnp.exp(-x))


def _dot(a, b):
    return jnp.dot(a, b, preferred_element_type=F32)


def _dot_nt(a, b):
    return lax.dot_general(a, b, (((1,), (1,)), ((), ())), preferred_element_type=F32)


def _dot_tn(a, b):
    return lax.dot_general(a, b, (((0,), (0,)), ((), ())), preferred_element_type=F32)


def _mods_kernel(c_ref, w_ref, b_ref, o_ref):
    c = c_ref[...]
    h = c * _sigmoid(c)
    o_ref[0] = jnp.dot(h, w_ref[0], preferred_element_type=F32,
                       precision=lax.Precision.HIGHEST) + b_ref[0]


def _ada_mods(cvec, ada_w, ada_b):
    depth, d, nine_d = ada_w.shape
    nblk = nine_d // d
    return pl.pallas_call(
        _mods_kernel,
        grid=(depth, nblk),
        in_specs=[pl.BlockSpec((8, d), lambda l, j: (0, 0)),
                  pl.BlockSpec((1, d, d), lambda l, j: (l, 0, j)),
                  pl.BlockSpec((1, 1, d), lambda l, j: (l, 0, j))],
        out_specs=pl.BlockSpec((1, 8, d), lambda l, j: (l, 0, j)),
        out_shape=jax.ShapeDtypeStruct((depth, 8, nine_d), F32),
        compiler_params=_cparams(("parallel", "parallel"), 32),
        name="ada_mods",
    )(cvec, ada_w, ada_b.reshape(depth, 1, nine_d))


def _ffn_kernel(x_ref, mods_ref, w1_ref, w3_ref, w2_ref, ln_ref, o_ref, *, mod_base, ln_row, alpha):
    x = x_ref[...]
    m = mods_ref[...]
    shift = m[mod_base:mod_base + 1]
    scale = m[mod_base + 1:mod_base + 2]
    gate = m[mod_base + 2:mod_base + 3]
    h = (_ln_rows(x) * (1.0 + scale) + shift).astype(BF16)
    a = _dot(h, w1_ref[...])
    b = _dot(h, w3_ref[...])
    g = (a * _sigmoid(a) * b).astype(BF16)
    y = _dot(g, w2_ref[...])
    z = alpha * x + 0.5 * gate * y
    ln = ln_ref[...]
    o_ref[...] = _ln_rows(z) * ln[ln_row:ln_row + 1] + ln[3 + ln_row:4 + ln_row]


def _ffn(x2, mods, mods_row, w1, w3, w2, ln, *, tm, mod_base, ln_row, alpha):
    t, d = x2.shape
    dff = w1.shape[1]
    kern = functools.partial(_ffn_kernel, mod_base=mod_base, ln_row=ln_row, alpha=alpha)
    return pl.pallas_call(
        kern,
        grid=(t // tm,),
        in_specs=[pl.BlockSpec((tm, d), lambda i: (i, 0)),
                  pl.BlockSpec((None, 9, d), lambda i: (mods_row(i), 0, 0)),
                  _resident((d, dff)), _resident((d, dff)), _resident((dff, d)),
                  _resident((6, d))],
        out_specs=pl.BlockSpec((tm, d), lambda i: (i, 0)),
        out_shape=jax.ShapeDtypeStruct((t, d), F32),
        compiler_params=_cparams(("parallel",), 56),
        name="ffn",
    )(x2, mods, w1, w3, w2, ln)


def _rope(t, cos, sin_signed, first_half):
    swapped = jnp.where(first_half, pltpu.roll(t, 96, 1), pltpu.roll(t, 32, 1))
    return t * cos + swapped * sin_signed


def _proj_kernel(x_ref, mods_ref, w_ref, cos_ref, sin_ref, cs_ref,
                 q_ref, k_ref, v_ref, g_ref, nq_ref, nk_ref, nv_ref, zre_ref, zim_ref, gt_ref,
                 *, q_scale, nq_scale):
    x = x_ref[...]
    m = mods_ref[...]
    h = (_ln_rows(x) * (1.0 + m[4:5]) + m[3:4]).astype(BF16)
    tm = x.shape[0]
    cw = 512

    def seg(j):
        return _dot(h, w_ref[:, j * cw:(j + 1) * cw])

    cos = cos_ref[...]
    sin = sin_ref[...]
    lane = lax.broadcasted_iota(jnp.int32, (tm, V7X_LANES), 1)
    first_half = (lane & 32) == 0

    r = seg(0) * q_scale
    for hh in range(RET_HEADS):
        sl = slice(hh * RET_DK, (hh + 1) * RET_DK)
        q_ref[:, sl] = _rope(r[:, sl], cos, sin, first_half).astype(BF16)
    r = seg(1)
    for hh in range(RET_HEADS):
        sl = slice(hh * RET_DK, (hh + 1) * RET_DK)
        k_ref[:, sl] = _rope(r[:, sl], cos, sin, first_half).astype(BF16)
    for j in range(2):
        v_ref[:, j * cw:(j + 1) * cw] = seg(2 + j).astype(BF16)
    for j in range(2):
        r = seg(4 + j)
        g_ref[:, j * cw:(j + 1) * cw] = (r * _sigmoid(r)).astype(BF16)
    nq_ref[...] = (seg(6) * nq_scale).astype(BF16)
    nk_ref[...] = seg(7).astype(BF16)
    nv_ref[...] = seg(8).astype(BF16)
    u = seg(9).astype(BF16)
    cs = cs_ref[...]
    for gi in range(FOU_GROUPS):
        sl = slice(gi * FOU_DG, (gi + 1) * FOU_DG)
        z = _dot(u[:, sl], cs)
        zre_ref[:, sl] = z[:, :FOU_DG].astype(BF16)
        zim_ref[:, sl] = z[:, FOU_DG:].astype(BF16)
    for j in range(6):
        gt_ref[:, j * cw:(j + 1) * cw] = _sigmoid(seg(10 + j)).astype(BF16)


def _proj(x2, mods, mods_row, w_in, cos, sin, rope_row, cs, *, tm):
    t, d = x2.shape
    d_in = w_in.shape[1]
    widths = (512, 512, 1024, 1024, 512, 512, 512, 512, 512, 3072)
    kern = functools.partial(_proj_kernel, q_scale=RET_DK ** -0.5, nq_scale=NA_DH ** -0.5)
    return pl.pallas_call(
        kern,
        grid=(t // tm,),
        in_specs=[pl.BlockSpec((tm, d), lambda i: (i, 0)),
                  pl.BlockSpec((None, 9, d), lambda i: (mods_row(i), 0, 0)),
                  _resident((d, d_in)),
                  pl.BlockSpec((tm, V7X_LANES), lambda i: (rope_row(i), 0)),
                  pl.BlockSpec((tm, V7X_LANES), lambda i: (rope_row(i), 0)),
                  _resident((FOU_DG, 2 * FOU_DG))],
        out_specs=[pl.BlockSpec((tm, w), lambda i: (i, 0)) for w in widths],
        out_shape=[jax.ShapeDtypeStruct((t, w), BF16) for w in widths],
        compiler_params=_cparams(("parallel",), 56),
        name="mixer_in_proj",
    )(x2, mods, w_in, cos, sin, cs)


def _log_sigmoid(x):
    return jnp.minimum(x, 0.0) - jnp.log(1.0 + jnp.exp(-jnp.abs(x)))


def _kv_update(k, v, lf, lb, idx):
    n = k.shape[0]
    kf = k.astype(F32)
    kk = jnp.concatenate([(kf * jnp.exp(lf * (n - 1.0 - idx))).astype(BF16),
                          (kf * jnp.exp(lb * idx)).astype(BF16)], axis=1)
    return _dot_tn(kk, v)


def _ret_kernel(*refs, n_chunks, chunk, with_prefix):
    if with_prefix:
        lg_ref, q_ref, k_ref, v_ref, g_ref, gn_ref, kc_ref, vc_ref, o_ref, u_ref, s_ref = refs
    else:
        lg_ref, q_ref, k_ref, v_ref, g_ref, gn_ref, o_ref, u_ref, s_ref = refs
    c = chunk
    dk, dv = RET_DK, RET_DV
    ls = _log_sigmoid(lg_ref[...])
    lf = ls[0, 0:1, :]
    lb = ls[1, 0:1, :]
    lf_v = jnp.concatenate([lf] * (dv // V7X_LANES), axis=1)
    lb_v = jnp.concatenate([lb] * (dv // V7X_LANES), axis=1)
    lf_c = jnp.concatenate([lf] * (c // V7X_LANES), axis=1)
    lb_c = jnp.concatenate([lb] * (c // V7X_LANES), axis=1)
    idx = lax.broadcasted_iota(jnp.int32, (c, V7X_LANES), 0).astype(F32)

    def upd(i, carry):
        r0 = pl.multiple_of(i * c, c)
        u_ref[i] = _kv_update(k_ref[pl.ds(r0, c), :], v_ref[pl.ds(r0, c), :], lf, lb, idx)
        return carry
    lax.fori_loop(0, n_chunks, upd, 0)

    if with_prefix:
        n_ctx = kc_ref.shape[0]
        idx_c = lax.broadcasted_iota(jnp.int32, (n_ctx, V7X_LANES), 0).astype(F32)
        u0 = _kv_update(kc_ref[...], vc_ref[...], lf, lb, idx_c)
        sf0, sb0 = u0[:dk], u0[dk:]
    else:
        sf0 = jnp.zeros((dk, dv), F32)
        sb0 = sf0
    gf_c = jnp.exp(lf_v * float(c))
    gb_c = jnp.exp(lb_v * float(c))

    def scan_f(i, sf):
        s_ref[i, :dk, :] = sf.astype(BF16)
        return sf * gf_c + u_ref[i, :dk, :]
    lax.fori_loop(0, n_chunks, scan_f, sf0)

    def scan_b(t, sb):
        i = n_chunks - 1 - t
        s_ref[i, dk:, :] = sb.astype(BF16)
        return sb * gb_c + u_ref[i, dk:, :]
    lax.fori_loop(0, n_chunks, scan_b, sb0)

    pr = (lax.broadcasted_iota(jnp.int32, (c, c), 0)
          - lax.broadcasted_iota(jnp.int32, (c, c), 1)).astype(F32)
    dmat = jnp.where(pr >= 0.0, jnp.exp(lf_c * jnp.maximum(pr, 0.0)),
                     jnp.exp(lb_c * jnp.maximum(-pr, 0.0)))
    qdf = jnp.exp(lf * (idx + 1.0))
    qdb = jnp.exp(lb * (float(c) - idx))
    gn = gn_ref[...]
    gn_w, gn_b = gn[0:1], gn[1:2]

    def out(i, carry):
        r0 = pl.multiple_of(i * c, c)
        q = q_ref[pl.ds(r0, c), :]
        k = k_ref[pl.ds(r0, c), :]
        v = v_ref[pl.ds(r0, c), :]
        inner = (_dot_nt(q, k) * dmat).astype(BF16)
        qf = q.astype(F32)
        qq = jnp.concatenate([(qf * qdf).astype(BF16), (qf * qdb).astype(BF16)], axis=1)
        o = _dot(inner, v) + _dot(qq, s_ref[i])
        o = _ln_rows(o) * gn_w + gn_b
        o_ref[pl.ds(r0, c), :] = (o * g_ref[pl.ds(r0, c), :].astype(F32)).astype(BF16)
        return carry
    lax.fori_loop(0, n_chunks, out, 0)


def _retention(lg, q, k, v, g, gn, kc=None, vc=None):
    b, n, _ = q.shape
    c = min(RET_CHUNK, n)
    nc = n // c
    with_prefix = kc is not None
    kern = functools.partial(_ret_kernel, n_chunks=nc, chunk=c, with_prefix=with_prefix)
    in_specs = [pl.BlockSpec((None, 2, 8, V7X_LANES), lambda bi, hi: (hi, 0, 0, 0)),
                pl.BlockSpec((None, n, RET_DK), lambda bi, hi: (bi, 0, hi)),
                pl.BlockSpec((None, n, RET_DK), lambda bi, hi: (bi, 0, hi)),
                pl.BlockSpec((None, n, RET_DV), lambda bi, hi: (bi, 0, hi)),
                pl.BlockSpec((None, n, RET_DV), lambda bi, hi: (bi, 0, hi)),
                pl.BlockSpec((None, 2, RET_DV), lambda bi, hi: (hi, 0, 0))]
    args = [lg, q, k, v, g, gn]
    if with_prefix:
        n_ctx = kc.shape[1]
        in_specs += [pl.BlockSpec((None, n_ctx, RET_DK), lambda bi, hi: (bi, 0, hi)),
                     pl.BlockSpec((None, n_ctx, RET_DV), lambda bi, hi: (bi, 0, hi))]
        args += [kc, vc]
    return pl.pallas_call(
        kern,
        grid=(b, RET_HEADS),
        in_specs=in_specs,
        out_specs=pl.BlockSpec((None, n, RET_DV), lambda bi, hi: (bi, 0, hi)),
        out_shape=jax.ShapeDtypeStruct((b, n, RET_HEADS * RET_DV), BF16),
        scratch_shapes=[pltpu.VMEM((nc, 2 * RET_DK, RET_DV), F32),
                        pltpu.VMEM((nc, 2 * RET_DK, RET_DV), BF16)],
        compiler_params=_cparams(("parallel", "parallel"), 56),
        name="retention",
    )(*args)


def _softmax_pv(s_parts, v_parts):
    m = None
    for s in s_parts:
        mi = jnp.max(s, axis=-1, keepdims=True)
        m = mi if m is None else jnp.maximum(m, mi)
    l = None
    o = None
    for s, v in zip(s_parts, v_parts):
        p = jnp.exp(s - m)
        li = jnp.sum(p, axis=-1, keepdims=True)
        oi = _dot(p.astype(BF16), v)
        l = li if l is None else l + li
        o = oi if o is None else o + oi
    return o / l


def _na_kernel(q_ref, k_ref, v_ref, kc_ref, vc_ref, bias_ref, o_ref, *, n_groups, key_row0_max):
    gq = NA_GROUP_ROWS * GRID_W
    nk = NA_KEY_ROWS * GRID_W
    lane = lax.broadcasted_iota(jnp.int32, (gq, V7X_LANES), 1)
    head0 = lane < NA_DH
    kc = kc_ref[...]
    vc = vc_ref[...]

    def group(g, carry):
        q0 = pl.multiple_of(g * gq, gq)
        kb = jnp.clip(g * NA_GROUP_ROWS - NA_KH // 2, 0, key_row0_max)
        k0 = pl.multiple_of(kb * GRID_W, GRID_W)
        cfg = jnp.where(g == 0, 0, jnp.where(g == n_groups - 1, 2, 1))
        q = q_ref[pl.ds(q0, gq), :]
        kl = k_ref[pl.ds(k0, nk), :]
        vl = v_ref[pl.ds(k0, nk), :]
        zero = jnp.zeros_like(q)
        outs = []
        for hh in range(2):
            qh = jnp.where(head0, q, zero) if hh == 0 else jnp.where(head0, zero, q)
            s_loc = _dot_nt(qh, kl) + bias_ref[cfg, hh]
            s_ctx = _dot_nt(qh, kc)
            outs.append(_softmax_pv([s_loc, s_ctx], [vl, vc]))
        o_ref[pl.ds(q0, gq), :] = jnp.where(head0, outs[0], outs[1]).astype(BF16)
        return carry
    lax.fori_loop(0, n_groups, group, 0)


def _na_bias_tables(rpb):
    h = rpb.shape[0]
    w = GRID_W
    cols = np.arange(w)
    cs = np.clip(cols - NA_KW // 2, 0, w - NA_KW)
    kcol = np.arange(w)[None, :]
    colmask = (kcol >= cs[:, None]) & (kcol < cs[:, None] + NA_KW)
    pad = w - 1
    rpb_p = jnp.pad(rpb, ((0, 0), (0, 0), (pad, pad)))
    toep = jnp.stack([rpb_p[:, :, pad + NA_KW - 1 - c: pad + NA_KW - 1 - c + w] for c in range(w)],
                     axis=2)
    toep = jnp.where(colmask[None, None], toep, NEG_BIG)
    neg = jnp.full((h, w, w), NEG_BIG, F32)
    tables = []
    for cfg in range(3):
        rel = NA_GROUP_ROWS * cfg
        rows_q = []
        for rr in range(NA_GROUP_ROWS):
            lo = (0, rr, NA_KEY_ROWS - NA_KH)[cfg]
            blks = []
            for i in range(NA_KEY_ROWS):
                if lo <= i < lo + NA_KH:
                    blks.append(toep[:, i - rel - rr + NA_KH - 1])
                else:
                    blks.append(neg)
            rows_q.append(jnp.concatenate(blks, axis=-1))
        tables.append(jnp.concatenate(rows_q, axis=1))
    return jnp.stack(tables, axis=0)


def _neighbourhood_attention(nq, nk, nv, kc, vc, bias):
    b, n, hw = nq.shape
    rows = n // GRID_W
    n_ctx = kc.shape[1]
    n_groups = rows // NA_GROUP_ROWS
    pairs = NA_HEADS // 2
    gq = NA_GROUP_ROWS * GRID_W
    nkeys = NA_KEY_ROWS * GRID_W
    kern = functools.partial(_na_kernel, n_groups=n_groups, key_row0_max=rows - NA_KEY_ROWS)
    tok = pl.BlockSpec((None, n, V7X_LANES), lambda bi, pi: (bi, 0, pi))
    ctx = pl.BlockSpec((None, n_ctx, V7X_LANES), lambda bi, pi: (bi, 0, pi))
    return pl.pallas_call(
        kern,
        grid=(b, pairs),
        in_specs=[tok, tok, tok, ctx, ctx,
                  pl.BlockSpec((3, 2, gq, nkeys), lambda bi, pi: (0, pi, 0, 0))],
        out_specs=tok,
        out_shape=jax.ShapeDtypeStruct((b, n, hw), BF16),
        compiler_params=_cparams(("parallel", "parallel"), 48),
        name="neighbourhood_attention",
    )(nq, nk, nv, kc, vc, bias)


def _ctx_attn_kernel(q_ref, k_ref, v_ref, o_ref):
    q = q_ref[...]
    k = k_ref[...]
    v = v_ref[...]
    lane = lax.broadcasted_iota(jnp.int32, q.shape, 1)
    head0 = lane < NA_DH
    zero = jnp.zeros_like(q)
    outs = []
    for hh in range(2):
        qh = jnp.where(head0, q, zero) if hh == 0 else jnp.where(head0, zero, q)
        outs.append(_softmax_pv([_dot_nt(qh, k)], [v]))
    o_ref[...] = jnp.where(head0, outs[0], outs[1]).astype(BF16)


def _context_attention(q, k, v):
    b, n_ctx, hw = q.shape
    blk = pl.BlockSpec((None, n_ctx, V7X_LANES), lambda bi, pi: (bi, 0, pi))
    return pl.pallas_call(
        _ctx_attn_kernel,
        grid=(b, NA_HEADS // 2),
        in_specs=[blk, blk, blk],
        out_specs=blk,
        out_shape=jax.ShapeDtypeStruct((b, n_ctx, hw), BF16),
        compiler_params=_cparams(("parallel", "parallel"), 32),
        name="context_attention",
    )(q, k, v)


def _dft_rows_kernel(zre_ref, zim_ref, w_ref, are_ref, aim_ref):
    r = zre_ref.shape[0]
    z = jnp.concatenate([zre_ref[...], zim_ref[...]], axis=0)
    a = _dot(w_ref[...], z)
    are_ref[...] = a[:r].astype(BF16)
    aim_ref[...] = a[r:].astype(BF16)


def _dft_rows(zre, zim, w2):
    b, r, wc = zre.shape
    cb = min(4096, wc)
    blk = pl.BlockSpec((None, r, cb), lambda bi, ci: (bi, 0, ci))
    return pl.pallas_call(
        _dft_rows_kernel,
        grid=(b, wc // cb),
        in_specs=[blk, blk, _resident((2 * r, 2 * r))],
        out_specs=[blk, blk],
        out_shape=[jax.ShapeDtypeStruct((b, r, wc), BF16)] * 2,
        compiler_params=_cparams(("parallel", "parallel"), 32),
        name="dft_rows",
    )(zre, zim, w2)


def _dft_cols_kernel(are_ref, aim_ref, twc_ref, tws_ref, km_ref, wo_ref, o_ref, *, scale):
    ar = are_ref[...].astype(F32)
    ai = aim_ref[...].astype(F32)
    reps = ar.shape[1] // V7X_LANES
    tc = jnp.concatenate([twc_ref[...]] * reps, axis=1)
    ts = jnp.concatenate([tws_ref[...]] * reps, axis=1)
    a = jnp.concatenate([(ar * tc - ai * ts).astype(BF16), (ar * ts + ai * tc).astype(BF16)], axis=0)
    y = _dot(km_ref[...], a) * scale
    out = _dot(y.astype(BF16), wo_ref[...])
    o_ref[...] = out.reshape(o_ref.shape)


def _dft_cols(a_re, a_im, twc, tws, kmat, w_out, *, scale):
    b, rw, ch = a_re.shape
    w = GRID_W
    r = rw // w
    d = w_out.shape[1]
    jb = FOU_K1_BLOCK
    blk = pl.BlockSpec((None, jb * w, ch), lambda bi, ji: (bi, ji, 0))
    tw = pl.BlockSpec((jb * w, V7X_LANES), lambda bi, ji: (ji, 0))
    kern = functools.partial(_dft_cols_kernel, scale=scale)
    return pl.pallas_call(
        kern,
        grid=(b, r // jb),
        in_specs=[blk, blk, tw, tw, _resident(kmat.shape), _resident(w_out.shape)],
        out_specs=pl.BlockSpec((None, w, jb, d), lambda bi, ji: (bi, 0, ji, 0)),
        out_shape=jax.ShapeDtypeStruct((b, w, r, d), F32),
        compiler_params=_cparams(("parallel", "parallel"), 32),
        name="dft_cols",
    )(a_re, a_im, twc, tws, kmat, w_out)


def _ctx_fourier_kernel(zre_ref, zim_ref, cm_ref, wo_ref, o_ref, *, scale):
    z = jnp.concatenate([zre_ref[...], zim_ref[...]], axis=0)
    y = _dot(cm_ref[...], z) * scale
    o_ref[...] = _dot(y.astype(BF16), wo_ref[...])


def _ctx_fourier(zre, zim, cmat, w_out, *, scale):
    b, n_ctx, ch = zre.shape
    d = w_out.shape[1]
    blk = pl.BlockSpec((None, n_ctx, ch), lambda bi: (bi, 0, 0))
    kern = functools.partial(_ctx_fourier_kernel, scale=scale)
    return pl.pallas_call(
        kern,
        grid=(b,),
        in_specs=[blk, blk, _resident(cmat.shape), _resident(w_out.shape)],
        out_specs=pl.BlockSpec((None, n_ctx, d), lambda bi: (bi, 0, 0)),
        out_shape=jax.ShapeDtypeStruct((b, n_ctx, d), F32),
        compiler_params=_cparams(("parallel",), 32),
        name="ctx_fourier",
    )(zre, zim, cmat, w_out)


def _merge_kernel(x_ref, mods_ref, ret_ref, na_ref, fou_ref, gt_ref, wr_ref, wn_ref, wo_ref, ln_ref,
                  o_ref, *, alpha):
    d = x_ref.shape[1]
    y_ret = _dot(ret_ref[...], wr_ref[...])
    y_na = _dot(na_ref[...], wn_ref[...])
    mix = (gt_ref[:, 0:d].astype(F32) * y_ret + gt_ref[:, d:2 * d].astype(F32) * y_na
           + gt_ref[:, 2 * d:3 * d].astype(F32) * fou_ref[...])
    y = _dot(mix.astype(BF16), wo_ref[...])
    m = mods_ref[...]
    z = alpha * x_ref[...] + m[5:6] * y
    ln = ln_ref[...]
    o_ref[...] = _ln_rows(z) * ln[1:2] + ln[4:5]


def _merge(x2, mods, mods_row, ret, na, fou, gt, wr, wn, wo, ln, *, tm, alpha):
    t, d = x2.shape
    kern = functools.partial(_merge_kernel, alpha=alpha)

    def rows(wd):
        return pl.BlockSpec((tm, wd), lambda i: (i, 0))
    return pl.pallas_call(
        kern,
        grid=(t // tm,),
        in_specs=[rows(d), pl.BlockSpec((None, 9, d), lambda i: (mods_row(i), 0, 0)),
                  rows(ret.shape[1]), rows(na.shape[1]), rows(d), rows(gt.shape[1]),
                  _resident(wr.shape), _resident(wn.shape), _resident(wo.shape), _resident((6, d))],
        out_specs=rows(d),
        out_shape=jax.ShapeDtypeStruct((t, d), F32),
        compiler_params=_cparams(("parallel",), 48),
        name="merge",
    )(x2, mods, ret, na, fou, gt, wr, wn, wo, ln)


def _rope_tables(n):
    half = RET_DK // 4
    t = jnp.arange(n)
    row, col = t // GRID_W, t % GRID_W
    freqs = ROPE_BASE ** (-jnp.arange(half, dtype=F32) / half)
    ang_r = row.astype(F32)[:, None] * freqs[None, :]
    ang_c = col.astype(F32)[:, None] * freqs[None, :]
    cos = jnp.concatenate([jnp.cos(ang_r)] * 2 + [jnp.cos(ang_c)] * 2, axis=1)
    sin = jnp.concatenate([-jnp.sin(ang_r), jnp.sin(ang_r), -jnp.sin(ang_c), jnp.sin(ang_c)], axis=1)
    return cos, sin


def _dft_tables(n):
    w = GRID_W
    r = n // w
    jb = FOU_K1_BLOCK
    two_pi = 2.0 * np.pi
    ch = np.arange(FOU_DG)
    ang = two_pi * ((ch[:, None] * ch[None, :]) % FOU_DG) / FOU_DG
    cs = np.concatenate([np.cos(ang), np.sin(ang)], axis=1)
    k1 = np.arange(r)
    ang = two_pi * ((k1[:, None] * k1[None, :]) % r) / r
    w2 = np.block([[np.cos(ang), -np.sin(ang)], [np.sin(ang), np.cos(ang)]])
    c = np.arange(w)
    ang = two_pi * (k1[:, None] * c[None, :]).reshape(-1) / n
    twc = np.repeat(np.cos(ang)[:, None], V7X_LANES, axis=1)
    tws = np.repeat(np.sin(ang)[:, None], V7X_LANES, axis=1)
    ang = two_pi * ((c[:, None] * c[None, :]) % w) / w
    eye = np.eye(jb)
    kc = np.einsum("kc,ji->kjic", np.cos(ang), eye).reshape(w * jb, jb * w)
    ks = np.einsum("kc,ji->kjic", np.sin(ang), eye).reshape(w * jb, jb * w)
    kmat = np.concatenate([kc, -ks], axis=1)
    as_bf16 = lambda a: jnp.asarray(a, F32).astype(BF16)
    return (as_bf16(cs), as_bf16(w2), jnp.asarray(twc, F32), jnp.asarray(tws, F32), as_bf16(kmat))


def _ctx_dft_matrix(n_ctx):
    m = np.arange(n_ctx)
    ang = 2.0 * np.pi * ((m[:, None] * m[None, :]) % n_ctx) / n_ctx
    return jnp.asarray(np.concatenate([np.cos(ang), -np.sin(ang)], axis=1), F32).astype(BF16)


def kernel(x, c, ctx, c_ctx, ada_w, ada_b, ln_w, ln_b, ffa_w1, ffa_w3, ffa_w2, mix_w_in,
           ret_decay_logit, ret_gn_w, ret_gn_b, ret_w_out, na_rpb, na_w_out, fou_w_out, mix_w_o,
           ffb_w1, ffb_w3, ffb_w2):
    b, n, d = x.shape
    n_ctx = ctx.shape[1]
    depth = ada_w.shape[0]
    rows = n // GRID_W
    assert n % TOKEN_TILE == 0 and rows % NA_GROUP_ROWS == 0 and rows >= NA_KEY_ROWS
    assert rows % FOU_K1_BLOCK == 0 and b + 1 <= 8
    alpha = (2 * depth) ** 0.25
    tm = TOKEN_TILE
    tiles_per_seq = n // tm
    lat_row = lambda i: i // tiles_per_seq
    ctx_row = lambda i: b
    lat_rope = lambda i: i % tiles_per_seq
    ctx_rope = lambda i: 0

    cos_x, sin_x = _rope_tables(n)
    cos_c = jnp.ones((n_ctx, V7X_LANES), F32)
    sin_c = jnp.zeros((n_ctx, V7X_LANES), F32)
    cs, w2, twc, tws, kmat = _dft_tables(n)
    cmat_ctx = _ctx_dft_matrix(n_ctx)
    fou_scale_x = 1.0 / math.sqrt(n * FOU_DG)
    fou_scale_c = 1.0 / math.sqrt(n_ctx * FOU_DG)

    cvec = jnp.concatenate([c, c_ctx[None, :], jnp.zeros((8 - b - 1, d), F32)], axis=0)
    mods_all = _ada_mods(cvec, ada_w, ada_b).reshape(depth, 8, 9, d)
    ln_all = jnp.concatenate([ln_w, ln_b], axis=1)
    to_bf16 = lambda a: a.astype(BF16)
    ffa = [to_bf16(a) for a in (ffa_w1, ffa_w3, ffa_w2)]
    ffb = [to_bf16(a) for a in (ffb_w1, ffb_w3, ffb_w2)]
    w_in, wr, wn, wf, wo = [to_bf16(a) for a in (mix_w_in, ret_w_out, na_w_out, fou_w_out, mix_w_o)]
    lg_all = jnp.broadcast_to(
        jnp.transpose(ret_decay_logit, (0, 2, 1))[:, :, :, None, None],
        (depth, RET_HEADS, 2, 8, V7X_LANES)).astype(F32)
    gn_all = jnp.stack([ret_gn_w.reshape(depth, RET_HEADS, RET_DV),
                        ret_gn_b.reshape(depth, RET_HEADS, RET_DV)], axis=2)

    xs = x.reshape(b * n, d)
    cx = ctx.reshape(b * n_ctx, d)
    for l in range(depth):
        last = l == depth - 1
        mods = mods_all[l]
        ln = ln_all[l]
        xs = _ffn(xs, mods, lat_row, ffa[0][l], ffa[1][l], ffa[2][l], ln, tm=tm, mod_base=0,
                  ln_row=0, alpha=alpha)
        cx = _ffn(cx, mods, ctx_row, ffa[0][l], ffa[1][l], ffa[2][l], ln, tm=n_ctx, mod_base=0,
                  ln_row=0, alpha=alpha)

        px = _proj(xs, mods, lat_row, w_in[l], cos_x, sin_x, lat_rope, cs, tm=tm)
        pc = _proj(cx, mods, ctx_row, w_in[l], cos_c, sin_c, ctx_rope, cs, tm=n_ctx)
        q_x, k_x, v_x, g_x, nq_x, nk_x, nv_x, zre_x, zim_x, gt_x = [a.reshape(b, n, -1) for a in px]
        q_c, k_c, v_c, g_c, nq_c, nk_c, nv_c, zre_c, zim_c, gt_c = [a.reshape(b, n_ctx, -1) for a in pc]

        ret_x = _retention(lg_all[l], q_x, k_x, v_x, g_x, gn_all[l], k_c, v_c)
        bias = _na_bias_tables(na_rpb[l])
        na_x = _neighbourhood_attention(nq_x, nk_x, nv_x, nk_c, nv_c, bias)
        a_re, a_im = _dft_rows(zre_x.reshape(b, rows, -1), zim_x.reshape(b, rows, -1), w2)
        fou_x = _dft_cols(a_re.reshape(b, n, -1), a_im.reshape(b, n, -1), twc, tws, kmat, wf[l],
                          scale=fou_scale_x)
        xs = _merge(xs, mods, lat_row, ret_x.reshape(b * n, -1), na_x.reshape(b * n, -1),
                    fou_x.reshape(b * n, d), gt_x.reshape(b * n, -1), wr[l], wn[l], wo[l], ln,
                    tm=tm, alpha=alpha)
        xs = _ffn(xs, mods, lat_row, ffb[0][l], ffb[1][l], ffb[2][l], ln, tm=tm, mod_base=6,
                  ln_row=2, alpha=alpha)
        if not last:
            ret_c = _retention(lg_all[l], q_c, k_c, v_c, g_c, gn_all[l])
            na_c = _context_attention(nq_c, nk_c, nv_c)
            fou_c = _ctx_fourier(zre_c, zim_c, cmat_ctx, wf[l], scale=fou_scale_c)
            cx = _merge(cx, mods, ctx_row, ret_c.reshape(b * n_ctx, -1), na_c.reshape(b * n_ctx, -1),
                        fou_c.reshape(b * n_ctx, d), gt_c.reshape(b * n_ctx, -1), wr[l], wn[l],
                        wo[l], ln, tm=n_ctx, alpha=alpha)
            cx = _ffn(cx, mods, ctx_row, ffb[0][l], ffb[1][l], ffb[2][l], ln, tm=n_ctx, mod_base=6,
                      ln_row=2, alpha=alpha)
    return xs.reshape(b, n, d)
```

```python
import functools
import math

import numpy as np
import jax
import jax.numpy as jnp
from jax import lax
from jax.experimental import pallas as pl
from jax.experimental.pallas import tpu as pltpu

F32 = jnp.float32
BF16 = jnp.bfloat16

GRID_W = 64
RET_HEADS = 4
RET_DK = 128
RET_DV = 256
NA_HEADS = 8
NA_DH = 64
NA_KH = 8
NA_KW = 16
FOU_GROUPS = 4
FOU_DG = 128
N_BRANCH = 3
ROPE_BASE = 10000.0
LN_EPS = 1e-6

V7X_VMEM_BYTES = 64 * 1024 * 1024
V7X_LANES = 128
V7X_SUBLANES = 8

TOKEN_TILE = 512
RET_CHUNK = 256
TOKEN_SPLIT = 2
SUBTILE_MIN = 256
RET_UNROLL = 4
NA_UNROLL = 1
NA_GROUP_ROWS = 4
NA_KEY_ROWS = NA_GROUP_ROWS + NA_KH
FOU_K1_BLOCK = V7X_SUBLANES
NEG_BIG = -1e30
LOG2_E = math.log2(math.e)


def _cparams(sem, vmem_mb):
    return pltpu.CompilerParams(dimension_semantics=sem,
                                vmem_limit_bytes=int(vmem_mb * 1024 * 1024))


def _resident(shape, layer=None):
    nd = len(shape)
    if layer is None:
        return pl.BlockSpec(shape, lambda *_: (0,) * nd, pipeline_mode=pl.Buffered(1))
    return pl.BlockSpec((None,) + tuple(shape), lambda *_: (layer,) + (0,) * nd,
                        pipeline_mode=pl.Buffered(1))


def _row_splits(tm):
    n = max(1, min(TOKEN_SPLIT, tm // SUBTILE_MIN))
    step = tm // n
    return [slice(s * step, (s + 1) * step) for s in range(n)]


def _ln_rows(x):
    mu = jnp.mean(x, axis=-1, keepdims=True)
    xc = x - mu
    var = jnp.mean(xc * xc, axis=-1, keepdims=True)
    return xc * lax.rsqrt(var + LN_EPS)


def _sigmoid(x):
    return 1.0 / (1.0 + jnp.exp(-x))


def _dot(a, b):
    return jnp.dot(a, b, preferred_element_type=F32)


def _dot_nt(a, b):
    return lax.dot_general(a, b, (((1,), (1,)), ((), ())), preferred_element_type=F32)


def _dot_tn(a, b):
    return lax.dot_general(a, b, (((0,), (0,)), ((), ())), preferred_element_type=F32)


def _mods_kernel(c_ref, w_ref, b_ref, o_ref):
    c = c_ref[...]
    h = c * _sigmoid(c)
    o_ref[0] = jnp.dot(h, w_ref[0], preferred_element_type=F32,
                       precision=lax.Precision.HIGHEST) + b_ref[0]


def _ada_mods(cvec, ada_w, ada_b):
    depth, d, nine_d = ada_w.shape
    nblk = nine_d // d
    return pl.pallas_call(
        _mods_kernel,
        grid=(depth, nblk),
        in_specs=[pl.BlockSpec((8, d), lambda l, j: (0, 0)),
                  pl.BlockSpec((1, d, d), lambda l, j: (l, 0, j)),
                  pl.BlockSpec((1, 1, d), lambda l, j: (l, 0, j))],
        out_specs=pl.BlockSpec((1, 8, d), lambda l, j: (l, 0, j)),
        out_shape=jax.ShapeDtypeStruct((depth, 8, nine_d), F32),
        compiler_params=_cparams(("parallel", "parallel"), 32),
        name="ada_mods",
    )(cvec, ada_w, ada_b.reshape(depth, 1, nine_d))


def _ffn_kernel(x_ref, mods_ref, w1_ref, w3_ref, w2_ref, ln_ref, o_ref, *, mod_base, ln_row, alpha):
    m = mods_ref[...]
    shift = m[mod_base:mod_base + 1]
    scale = m[mod_base + 1:mod_base + 2]
    gate = m[mod_base + 2:mod_base + 3]
    ln = ln_ref[...]
    for rows in _row_splits(x_ref.shape[0]):
        x = x_ref[rows, :]
        h = (_ln_rows(x) * (1.0 + scale) + shift).astype(BF16)
        a = _dot(h, w1_ref[...])
        b = _dot(h, w3_ref[...])
        g = (a * _sigmoid(a) * b).astype(BF16)
        y = _dot(g, w2_ref[...])
        z = alpha * x + 0.5 * gate * y
        o_ref[rows, :] = _ln_rows(z) * ln[ln_row:ln_row + 1] + ln[3 + ln_row:4 + ln_row]


def _ffn(x2, mods, mods_row, w1, w3, w2, ln, *, layer, tm, mod_base, ln_row, alpha):
    t, d = x2.shape
    dff = w1.shape[2]
    kern = functools.partial(_ffn_kernel, mod_base=mod_base, ln_row=ln_row, alpha=alpha)
    return pl.pallas_call(
        kern,
        grid=(t // tm,),
        in_specs=[pl.BlockSpec((tm, d), lambda i: (i, 0)),
                  pl.BlockSpec((None, None, 9, d), lambda i: (layer, mods_row(i), 0, 0)),
                  _resident((d, dff), layer), _resident((d, dff), layer), _resident((dff, d), layer),
                  _resident((6, d), layer)],
        out_specs=pl.BlockSpec((tm, d), lambda i: (i, 0)),
        out_shape=jax.ShapeDtypeStruct((t, d), F32),
        compiler_params=_cparams(("parallel",), 56),
        name="ffn",
    )(x2, mods, w1, w3, w2, ln)


def _rope(t, cos, sin_signed, first_half):
    swapped = jnp.where(first_half, pltpu.roll(t, 96, 1), pltpu.roll(t, 32, 1))
    return t * cos + swapped * sin_signed


def _proj_kernel(x_ref, mods_ref, w_ref, cos_ref, sin_ref, cs_ref, *rest,
                 q_scale, nq_scale, grid_rows_out):
    extra = rest[:-10]
    q_ref, k_ref, v_ref, g_ref, nq_ref, nk_ref, nv_ref, zre_ref, zim_ref, gt_ref = rest[-10:]
    x = x_ref[...]
    m = mods_ref[...]
    h = (_ln_rows(x) * (1.0 + m[4:5]) + m[3:4]).astype(BF16)
    tm = x.shape[0]
    cw = 512

    def seg(j):
        return _dot(h, w_ref[:, j * cw:(j + 1) * cw])

    cos = cos_ref[...]
    sin = sin_ref[...]
    lane = lax.broadcasted_iota(jnp.int32, (tm, V7X_LANES), 1)
    first_half = (lane & 32) == 0

    r = seg(0) * q_scale
    for hh in range(RET_HEADS):
        sl = slice(hh * RET_DK, (hh + 1) * RET_DK)
        q_ref[:, sl] = _rope(r[:, sl], cos, sin, first_half).astype(BF16)
    r = seg(1)
    for hh in range(RET_HEADS):
        sl = slice(hh * RET_DK, (hh + 1) * RET_DK)
        k_ref[:, sl] = _rope(r[:, sl], cos, sin, first_half).astype(BF16)
    for j in range(2):
        v_ref[:, j * cw:(j + 1) * cw] = seg(2 + j).astype(BF16)
    for j in range(2):
        r = seg(4 + j)
        g_ref[:, j * cw:(j + 1) * cw] = (r * _sigmoid(r)).astype(BF16)
    nq_ref[...] = (seg(6) * nq_scale).astype(BF16)
    nk_ref[...] = seg(7).astype(BF16)
    nv_ref[...] = seg(8).astype(BF16)
    u = seg(9).astype(BF16)
    cs = cs_ref[...]
    if grid_rows_out:
        perm_ref, = extra
        u = _dot(perm_ref[...], u).astype(BF16)
        n_r = tm // GRID_W
        for gi in range(FOU_GROUPS):
            sl = slice(gi * FOU_DG, (gi + 1) * FOU_DG)
            z = _dot(u[:, sl], cs)
            for c in range(GRID_W):
                dst = slice((c * FOU_GROUPS + gi) * FOU_DG, (c * FOU_GROUPS + gi + 1) * FOU_DG)
                zre_ref[:, dst] = z[c * n_r:(c + 1) * n_r, :FOU_DG]
                zim_ref[:, dst] = z[c * n_r:(c + 1) * n_r, FOU_DG:]
    else:
        for gi in range(FOU_GROUPS):
            sl = slice(gi * FOU_DG, (gi + 1) * FOU_DG)
            z = _dot(u[:, sl], cs)
            zre_ref[:, sl] = z[:, :FOU_DG].astype(BF16)
            zim_ref[:, sl] = z[:, FOU_DG:].astype(BF16)
    for j in range(6):
        gt_ref[:, j * cw:(j + 1) * cw] = _sigmoid(seg(10 + j)).astype(BF16)


def _proj(x2, mods, mods_row, w_in, cos, sin, rope_row, cs, *, layer, tm, grid_rows_out):
    t, d = x2.shape
    d_in = w_in.shape[2]
    ch = FOU_GROUPS * FOU_DG
    widths = (512, 512, 1024, 1024, 512, 512, 512, ch, ch, 3072)
    kern = functools.partial(_proj_kernel, q_scale=RET_DK ** -0.5, nq_scale=NA_DH ** -0.5 * LOG2_E,
                             grid_rows_out=grid_rows_out)
    out_specs = [pl.BlockSpec((tm, w), lambda i: (i, 0)) for w in widths]
    out_shape = [jax.ShapeDtypeStruct((t, w), BF16) for w in widths]
    in_specs = [pl.BlockSpec((tm, d), lambda i: (i, 0)),
                pl.BlockSpec((None, None, 9, d), lambda i: (layer, mods_row(i), 0, 0)),
                _resident((d, d_in), layer),
                pl.BlockSpec((tm, V7X_LANES), lambda i: (rope_row(i), 0)),
                pl.BlockSpec((tm, V7X_LANES), lambda i: (rope_row(i), 0)),
                _resident((FOU_DG, 2 * FOU_DG))]
    args = [x2, mods, w_in, cos, sin, cs]
    if grid_rows_out:
        n_r = tm // GRID_W
        for o in (7, 8):
            out_specs[o] = pl.BlockSpec((n_r, GRID_W * ch), lambda i: (i, 0))
            out_shape[o] = jax.ShapeDtypeStruct((t // GRID_W, GRID_W * ch), F32)
        src = (np.arange(n_r)[None, :] * GRID_W + np.arange(GRID_W)[:, None]).reshape(-1)
        perm = np.zeros((tm, tm), np.float32)
        perm[np.arange(tm), src] = 1.0
        in_specs.append(_resident((tm, tm)))
        args.append(jnp.asarray(perm).astype(BF16))
    return pl.pallas_call(
        kern,
        grid=(t // tm,),
        in_specs=in_specs,
        out_specs=out_specs,
        out_shape=out_shape,
        compiler_params=_cparams(("parallel",), 56),
        name="mixer_in_proj",
    )(*args)


def _log_sigmoid(x):
    return jnp.minimum(x, 0.0) - jnp.log(1.0 + jnp.exp(-jnp.abs(x)))


def _kv_update(k, v, lf, lb, idx):
    n = k.shape[0]
    kf = k.astype(F32)
    kk = jnp.concatenate([(kf * jnp.exp(lf * (n - 1.0 - idx))).astype(BF16),
                          (kf * jnp.exp(lb * idx)).astype(BF16)], axis=1)
    return _dot_tn(kk, v)


def _ret_kernel(*refs, n_chunks, chunk, with_prefix):
    if with_prefix:
        lg_ref, q_ref, k_ref, v_ref, g_ref, gn_ref, kc_ref, vc_ref, o_ref, u_ref, s_ref = refs
    else:
        lg_ref, q_ref, k_ref, v_ref, g_ref, gn_ref, o_ref, u_ref, s_ref = refs
    c = chunk
    dk, dv = RET_DK, RET_DV
    ls = _log_sigmoid(lg_ref[...])
    lf = ls[0, 0:1, :]
    lb = ls[1, 0:1, :]
    lf_v = jnp.concatenate([lf] * (dv // V7X_LANES), axis=1)
    lb_v = jnp.concatenate([lb] * (dv // V7X_LANES), axis=1)
    lf_c = jnp.concatenate([lf] * (c // V7X_LANES), axis=1)
    lb_c = jnp.concatenate([lb] * (c // V7X_LANES), axis=1)
    idx = lax.broadcasted_iota(jnp.int32, (c, V7X_LANES), 0).astype(F32)

    def upd(i, carry):
        r0 = pl.multiple_of(i * c, c)
        u_ref[i] = _kv_update(k_ref[pl.ds(r0, c), :], v_ref[pl.ds(r0, c), :], lf, lb, idx)
        return carry
    lax.fori_loop(0, n_chunks, upd, 0, unroll=min(RET_UNROLL, n_chunks))

    if with_prefix:
        n_ctx = kc_ref.shape[0]
        idx_c = lax.broadcasted_iota(jnp.int32, (n_ctx, V7X_LANES), 0).astype(F32)
        u0 = _kv_update(kc_ref[...], vc_ref[...], lf, lb, idx_c)
        sf0, sb0 = u0[:dk], u0[dk:]
    else:
        sf0 = jnp.zeros((dk, dv), F32)
        sb0 = sf0
    gf_c = jnp.exp(lf_v * float(c))
    gb_c = jnp.exp(lb_v * float(c))

    def scan_f(i, sf):
        s_ref[i, :dk, :] = sf.astype(BF16)
        return sf * gf_c + u_ref[i, :dk, :]
    lax.fori_loop(0, n_chunks, scan_f, sf0)

    def scan_b(t, sb):
        i = n_chunks - 1 - t
        s_ref[i, dk:, :] = sb.astype(BF16)
        return sb * gb_c + u_ref[i, dk:, :]
    lax.fori_loop(0, n_chunks, scan_b, sb0)

    pr = (lax.broadcasted_iota(jnp.int32, (c, c), 0)
          - lax.broadcasted_iota(jnp.int32, (c, c), 1)).astype(F32)
    dmat = jnp.where(pr >= 0.0, jnp.exp(lf_c * jnp.maximum(pr, 0.0)),
                     jnp.exp(lb_c * jnp.maximum(-pr, 0.0)))
    qdf = jnp.exp(lf * (idx + 1.0))
    qdb = jnp.exp(lb * (float(c) - idx))
    gn = gn_ref[...]
    gn_w, gn_b = gn[0:1], gn[1:2]

    def out(i, carry):
        r0 = pl.multiple_of(i * c, c)
        q = q_ref[pl.ds(r0, c), :]
        k = k_ref[pl.ds(r0, c), :]
        v = v_ref[pl.ds(r0, c), :]
        inner = (_dot_nt(q, k) * dmat).astype(BF16)
        qf = q.astype(F32)
        qq = jnp.concatenate([(qf * qdf).astype(BF16), (qf * qdb).astype(BF16)], axis=1)
        o = _dot(inner, v) + _dot(qq, s_ref[i])
        o = _ln_rows(o) * gn_w + gn_b
        o_ref[pl.ds(r0, c), :] = (o * g_ref[pl.ds(r0, c), :].astype(F32)).astype(BF16)
        return carry
    lax.fori_loop(0, n_chunks, out, 0, unroll=min(RET_UNROLL, n_chunks))


def _retention(lg, q, k, v, g, gn, kc=None, vc=None, *, layer):
    b, n, _ = q.shape
    c = min(RET_CHUNK, n)
    nc = n // c
    with_prefix = kc is not None
    kern = functools.partial(_ret_kernel, n_chunks=nc, chunk=c, with_prefix=with_prefix)
    in_specs = [pl.BlockSpec((None, None, 2, 8, V7X_LANES), lambda bi, hi: (layer, hi, 0, 0, 0)),
                pl.BlockSpec((None, n, RET_DK), lambda bi, hi: (bi, 0, hi)),
                pl.BlockSpec((None, n, RET_DK), lambda bi, hi: (bi, 0, hi)),
                pl.BlockSpec((None, n, RET_DV), lambda bi, hi: (bi, 0, hi)),
                pl.BlockSpec((None, n, RET_DV), lambda bi, hi: (bi, 0, hi)),
                pl.BlockSpec((None, None, 2, RET_DV), lambda bi, hi: (layer, hi, 0, 0))]
    args = [lg, q, k, v, g, gn]
    if with_prefix:
        n_ctx = kc.shape[1]
        in_specs += [pl.BlockSpec((None, n_ctx, RET_DK), lambda bi, hi: (bi, 0, hi)),
                     pl.BlockSpec((None, n_ctx, RET_DV), lambda bi, hi: (bi, 0, hi))]
        args += [kc, vc]
    return pl.pallas_call(
        kern,
        grid=(b, RET_HEADS),
        in_specs=in_specs,
        out_specs=pl.BlockSpec((None, n, RET_DV), lambda bi, hi: (bi, 0, hi)),
        out_shape=jax.ShapeDtypeStruct((b, n, RET_HEADS * RET_DV), BF16),
        scratch_shapes=[pltpu.VMEM((nc, 2 * RET_DK, RET_DV), F32),
                        pltpu.VMEM((nc, 2 * RET_DK, RET_DV), BF16)],
        compiler_params=_cparams(("parallel", "parallel"), 56),
        name="retention",
    )(*args)


def _softmax_pv(s_parts, v_parts):
    m = None
    for s in s_parts:
        mi = jnp.max(s, axis=-1, keepdims=True)
        m = mi if m is None else jnp.maximum(m, mi)
    l = None
    o = None
    for s, v in zip(s_parts, v_parts):
        p = jnp.exp2(s - m)
        li = jnp.sum(p, axis=-1, keepdims=True)
        oi = _dot(p.astype(BF16), v)
        l = li if l is None else l + li
        o = oi if o is None else o + oi
    return o / l


def _na_kernel(q_ref, k_ref, v_ref, kc_ref, vc_ref, bias_ref, o_ref, *, n_groups, key_row0_max):
    gq = NA_GROUP_ROWS * GRID_W
    nk = NA_KEY_ROWS * GRID_W
    lane = lax.broadcasted_iota(jnp.int32, (gq, V7X_LANES), 1)
    head0 = lane < NA_DH
    kc = kc_ref[...]
    vc = vc_ref[...]

    def group(g, carry):
        q0 = pl.multiple_of(g * gq, gq)
        kb = jnp.clip(g * NA_GROUP_ROWS - NA_KH // 2, 0, key_row0_max)
        k0 = pl.multiple_of(kb * GRID_W, GRID_W)
        cfg = jnp.where(g == 0, 0, jnp.where(g == n_groups - 1, 2, 1))
        q = q_ref[pl.ds(q0, gq), :]
        kl = k_ref[pl.ds(k0, nk), :]
        vl = v_ref[pl.ds(k0, nk), :]
        zero = jnp.zeros_like(q)
        outs = []
        for hh in range(2):
            qh = jnp.where(head0, q, zero) if hh == 0 else jnp.where(head0, zero, q)
            s_loc = _dot_nt(qh, kl) + bias_ref[cfg, hh]
            s_ctx = _dot_nt(qh, kc)
            outs.append(_softmax_pv([s_loc, s_ctx], [vl, vc]))
        o_ref[pl.ds(q0, gq), :] = jnp.where(head0, outs[0], outs[1]).astype(BF16)
        return carry
    lax.fori_loop(0, n_groups, group, 0, unroll=min(NA_UNROLL, n_groups))


def _na_bias_tables(rpb):
    h = rpb.shape[0]
    w = GRID_W
    cols = np.arange(w)
    cs = np.clip(cols - NA_KW // 2, 0, w - NA_KW)
    kcol = np.arange(w)[None, :]
    colmask = (kcol >= cs[:, None]) & (kcol < cs[:, None] + NA_KW)
    pad = w - 1
    rpb_p = jnp.pad(rpb, ((0, 0), (0, 0), (pad, pad)))
    toep = jnp.stack([rpb_p[:, :, pad + NA_KW - 1 - c: pad + NA_KW - 1 - c + w] for c in range(w)],
                     axis=2)
    toep = jnp.where(colmask[None, None], toep * LOG2_E, NEG_BIG)
    neg = jnp.full((h, w, w), NEG_BIG, F32)
    tables = []
    for cfg in range(3):
        rel = NA_GROUP_ROWS * cfg
        rows_q = []
        for rr in range(NA_GROUP_ROWS):
            lo = (0, rr, NA_KEY_ROWS - NA_KH)[cfg]
            blks = []
            for i in range(NA_KEY_ROWS):
                if lo <= i < lo + NA_KH:
                    blks.append(toep[:, i - rel - rr + NA_KH - 1])
                else:
                    blks.append(neg)
            rows_q.append(jnp.concatenate(blks, axis=-1))
        tables.append(jnp.concatenate(rows_q, axis=1))
    return jnp.stack(tables, axis=0)


def _neighbourhood_attention(nq, nk, nv, kc, vc, bias):
    b, n, hw = nq.shape
    rows = n // GRID_W
    n_ctx = kc.shape[1]
    n_groups = rows // NA_GROUP_ROWS
    pairs = NA_HEADS // 2
    gq = NA_GROUP_ROWS * GRID_W
    nkeys = NA_KEY_ROWS * GRID_W
    kern = functools.partial(_na_kernel, n_groups=n_groups, key_row0_max=rows - NA_KEY_ROWS)
    tok = pl.BlockSpec((None, n, V7X_LANES), lambda bi, pi: (bi, 0, pi))
    ctx = pl.BlockSpec((None, n_ctx, V7X_LANES), lambda bi, pi: (bi, 0, pi))
    return pl.pallas_call(
        kern,
        grid=(b, pairs),
        in_specs=[tok, tok, tok, ctx, ctx,
                  pl.BlockSpec((3, 2, gq, nkeys), lambda bi, pi: (0, pi, 0, 0))],
        out_specs=tok,
        out_shape=jax.ShapeDtypeStruct((b, n, hw), BF16),
        compiler_params=_cparams(("parallel", "parallel"), 48),
        name="neighbourhood_attention",
    )(nq, nk, nv, kc, vc, bias)


def _ctx_attn_kernel(q_ref, k_ref, v_ref, o_ref):
    q = q_ref[...]
    k = k_ref[...]
    v = v_ref[...]
    lane = lax.broadcasted_iota(jnp.int32, q.shape, 1)
    head0 = lane < NA_DH
    zero = jnp.zeros_like(q)
    outs = []
    for hh in range(2):
        qh = jnp.where(head0, q, zero) if hh == 0 else jnp.where(head0, zero, q)
        outs.append(_softmax_pv([_dot_nt(qh, k)], [v]))
    o_ref[...] = jnp.where(head0, outs[0], outs[1]).astype(BF16)


def _context_attention(q, k, v):
    b, n_ctx, hw = q.shape
    blk = pl.BlockSpec((None, n_ctx, V7X_LANES), lambda bi, pi: (bi, 0, pi))
    return pl.pallas_call(
        _ctx_attn_kernel,
        grid=(b, NA_HEADS // 2),
        in_specs=[blk, blk, blk],
        out_specs=blk,
        out_shape=jax.ShapeDtypeStruct((b, n_ctx, hw), BF16),
        compiler_params=_cparams(("parallel", "parallel"), 32),
        name="context_attention",
    )(q, k, v)


def _dft_rows_kernel(zre_ref, zim_ref, w_ref, are_ref, aim_ref):
    r = zre_ref.shape[0]
    z = jnp.concatenate([zre_ref[...].astype(BF16), zim_ref[...].astype(BF16)], axis=0)
    a = _dot(w_ref[...], z)
    are_ref[...] = a[:r]
    aim_ref[...] = a[r:]


def _dft_rows(zre, zim, w2):
    b, r, wc = zre.shape
    cb = min(4096, wc)
    blk = pl.BlockSpec((None, r, cb), lambda bi, ci: (bi, 0, ci))
    return pl.pallas_call(
        _dft_rows_kernel,
        grid=(b, wc // cb),
        in_specs=[blk, blk, _resident((2 * r, 2 * r))],
        out_specs=[blk, blk],
        out_shape=[jax.ShapeDtypeStruct((b, r, wc), F32)] * 2,
        compiler_params=_cparams(("parallel", "parallel"), 40),
        name="dft_rows",
    )(zre, zim, w2)


def _dft_cols_kernel(are_ref, aim_ref, twc_ref, tws_ref, km_ref, wo_ref, o_ref, *, scale, ch):
    w = are_ref.shape[1] // ch
    ar = jnp.concatenate([are_ref[:, c * ch:(c + 1) * ch] for c in range(w)], axis=0)
    ai = jnp.concatenate([aim_ref[:, c * ch:(c + 1) * ch] for c in range(w)], axis=0)
    reps = ch // V7X_LANES
    tc = jnp.concatenate([twc_ref[...]] * reps, axis=1)
    ts = jnp.concatenate([tws_ref[...]] * reps, axis=1)
    a = jnp.concatenate([(ar * tc - ai * ts).astype(BF16), (ar * ts + ai * tc).astype(BF16)], axis=0)
    y = _dot(km_ref[...], a) * scale
    out = _dot(y.astype(BF16), wo_ref[...])
    o_ref[...] = out.reshape(o_ref.shape)


def _dft_cols(a_re, a_im, twc, tws, kmat, w_out, *, layer, scale):
    b, r, wc = a_re.shape
    w = GRID_W
    ch = wc // w
    d = w_out.shape[2]
    jb = FOU_K1_BLOCK
    blk = pl.BlockSpec((None, jb, wc), lambda bi, ji: (bi, ji, 0))
    tw = pl.BlockSpec((jb * w, V7X_LANES), lambda bi, ji: (ji, 0))
    kern = functools.partial(_dft_cols_kernel, scale=scale, ch=ch)
    return pl.pallas_call(
        kern,
        grid=(b, r // jb),
        in_specs=[blk, blk, tw, tw, _resident(kmat.shape), _resident((ch, d), layer)],
        out_specs=pl.BlockSpec((None, w, jb, d), lambda bi, ji: (bi, 0, ji, 0)),
        out_shape=jax.ShapeDtypeStruct((b, w, r, d), F32),
        compiler_params=_cparams(("parallel", "parallel"), 32),
        name="dft_cols",
    )(a_re, a_im, twc, tws, kmat, w_out)


def _ctx_fourier_kernel(zre_ref, zim_ref, cm_ref, wo_ref, o_ref, *, scale):
    z = jnp.concatenate([zre_ref[...], zim_ref[...]], axis=0)
    y = _dot(cm_ref[...], z) * scale
    o_ref[...] = _dot(y.astype(BF16), wo_ref[...])


def _ctx_fourier(zre, zim, cmat, w_out, *, layer, scale):
    b, n_ctx, ch = zre.shape
    d = w_out.shape[2]
    blk = pl.BlockSpec((None, n_ctx, ch), lambda bi: (bi, 0, 0))
    kern = functools.partial(_ctx_fourier_kernel, scale=scale)
    return pl.pallas_call(
        kern,
        grid=(b,),
        in_specs=[blk, blk, _resident(cmat.shape), _resident((ch, d), layer)],
        out_specs=pl.BlockSpec((None, n_ctx, d), lambda bi: (bi, 0, 0)),
        out_shape=jax.ShapeDtypeStruct((b, n_ctx, d), F32),
        compiler_params=_cparams(("parallel",), 32),
        name="ctx_fourier",
    )(zre, zim, cmat, w_out)


def _merge_kernel(x_ref, mods_ref, ret_ref, na_ref, fou_ref, gt_ref, wr_ref, wn_ref, wo_ref, ln_ref,
                  o_ref, *, alpha):
    d = x_ref.shape[1]
    y_ret = _dot(ret_ref[...], wr_ref[...])
    y_na = _dot(na_ref[...], wn_ref[...])
    mix = (gt_ref[:, 0:d].astype(F32) * y_ret + gt_ref[:, d:2 * d].astype(F32) * y_na
           + gt_ref[:, 2 * d:3 * d].astype(F32) * fou_ref[...])
    y = _dot(mix.astype(BF16), wo_ref[...])
    m = mods_ref[...]
    z = alpha * x_ref[...] + m[5:6] * y
    ln = ln_ref[...]
    o_ref[...] = _ln_rows(z) * ln[1:2] + ln[4:5]


def _merge(x2, mods, mods_row, ret, na, fou, gt, wr, wn, wo, ln, *, layer, tm, alpha):
    t, d = x2.shape
    kern = functools.partial(_merge_kernel, alpha=alpha)

    def rows(wd):
        return pl.BlockSpec((tm, wd), lambda i: (i, 0))
    return pl.pallas_call(
        kern,
        grid=(t // tm,),
        in_specs=[rows(d), pl.BlockSpec((None, None, 9, d), lambda i: (layer, mods_row(i), 0, 0)),
                  rows(ret.shape[1]), rows(na.shape[1]), rows(d), rows(gt.shape[1]),
                  _resident(wr.shape[1:], layer), _resident(wn.shape[1:], layer),
                  _resident(wo.shape[1:], layer), _resident((6, d), layer)],
        out_specs=rows(d),
        out_shape=jax.ShapeDtypeStruct((t, d), F32),
        compiler_params=_cparams(("parallel",), 48),
        name="merge",
    )(x2, mods, ret, na, fou, gt, wr, wn, wo, ln)


def _rope_tables(n):
    half = RET_DK // 4
    t = jnp.arange(n)
    row, col = t // GRID_W, t % GRID_W
    freqs = ROPE_BASE ** (-jnp.arange(half, dtype=F32) / half)
    ang_r = row.astype(F32)[:, None] * freqs[None, :]
    ang_c = col.astype(F32)[:, None] * freqs[None, :]
    cos = jnp.concatenate([jnp.cos(ang_r)] * 2 + [jnp.cos(ang_c)] * 2, axis=1)
    sin = jnp.concatenate([-jnp.sin(ang_r), jnp.sin(ang_r), -jnp.sin(ang_c), jnp.sin(ang_c)], axis=1)
    return cos, sin


def _dft_tables(n):
    w = GRID_W
    r = n // w
    jb = FOU_K1_BLOCK
    two_pi = 2.0 * np.pi
    ch = np.arange(FOU_DG)
    ang = two_pi * ((ch[:, None] * ch[None, :]) % FOU_DG) / FOU_DG
    cs = np.concatenate([np.cos(ang), np.sin(ang)], axis=1)
    k1 = np.arange(r)
    ang = two_pi * ((k1[:, None] * k1[None, :]) % r) / r
    w2 = np.block([[np.cos(ang), -np.sin(ang)], [np.sin(ang), np.cos(ang)]])
    c = np.arange(w)
    k1_blocks = k1.reshape(r // jb, 1, jb)
    ang = two_pi * (k1_blocks * c[None, :, None]).reshape(-1) / n
    twc = np.repeat(np.cos(ang)[:, None], V7X_LANES, axis=1)
    tws = np.repeat(np.sin(ang)[:, None], V7X_LANES, axis=1)
    ang = two_pi * ((c[:, None] * c[None, :]) % w) / w
    eye = np.eye(jb)
    kc = np.einsum("kc,ji->kjci", np.cos(ang), eye).reshape(w * jb, w * jb)
    ks = np.einsum("kc,ji->kjci", np.sin(ang), eye).reshape(w * jb, w * jb)
    kmat = np.concatenate([kc, -ks], axis=1)
    as_bf16 = lambda a: jnp.asarray(a, F32).astype(BF16)
    return (as_bf16(cs), as_bf16(w2), jnp.asarray(twc, F32), jnp.asarray(tws, F32), as_bf16(kmat))


def _ctx_dft_matrix(n_ctx):
    m = np.arange(n_ctx)
    ang = 2.0 * np.pi * ((m[:, None] * m[None, :]) % n_ctx) / n_ctx
    return jnp.asarray(np.concatenate([np.cos(ang), -np.sin(ang)], axis=1), F32).astype(BF16)


def kernel(x, c, ctx, c_ctx, ada_w, ada_b, ln_w, ln_b, ffa_w1, ffa_w3, ffa_w2, mix_w_in,
           ret_decay_logit, ret_gn_w, ret_gn_b, ret_w_out, na_rpb, na_w_out, fou_w_out, mix_w_o,
           ffb_w1, ffb_w3, ffb_w2):
    b, n, d = x.shape
    n_ctx = ctx.shape[1]
    depth = ada_w.shape[0]
    rows = n // GRID_W
    assert n % TOKEN_TILE == 0 and rows % NA_GROUP_ROWS == 0 and rows >= NA_KEY_ROWS
    assert rows % FOU_K1_BLOCK == 0 and b + 1 <= 8
    alpha = (2 * depth) ** 0.25
    tm = TOKEN_TILE
    tiles_per_seq = n // tm
    lat_row = lambda i: i // tiles_per_seq
    ctx_row = lambda i: b
    lat_rope = lambda i: i % tiles_per_seq
    ctx_rope = lambda i: 0

    cos_x, sin_x = _rope_tables(n)
    cos_c = jnp.ones((n_ctx, V7X_LANES), F32)
    sin_c = jnp.zeros((n_ctx, V7X_LANES), F32)
    cs, w2, twc, tws, kmat = _dft_tables(n)
    cmat_ctx = _ctx_dft_matrix(n_ctx)
    fou_scale_x = 1.0 / math.sqrt(n * FOU_DG)
    fou_scale_c = 1.0 / math.sqrt(n_ctx * FOU_DG)

    cvec = jnp.concatenate([c, c_ctx[None, :], jnp.zeros((8 - b - 1, d), F32)], axis=0)
    mods_all = _ada_mods(cvec, ada_w, ada_b).reshape(depth, 8, 9, d)
    ln_all = jnp.concatenate([ln_w, ln_b], axis=1)
    to_bf16 = lambda a: a.astype(BF16)
    ffa = [to_bf16(a) for a in (ffa_w1, ffa_w3, ffa_w2)]
    ffb = [to_bf16(a) for a in (ffb_w1, ffb_w3, ffb_w2)]
    w_in, wr, wn, wf, wo = [to_bf16(a) for a in (mix_w_in, ret_w_out, na_w_out, fou_w_out, mix_w_o)]
    lg_all = jnp.broadcast_to(
        jnp.transpose(ret_decay_logit, (0, 2, 1))[:, :, :, None, None],
        (depth, RET_HEADS, 2, 8, V7X_LANES)).astype(F32)
    gn_all = jnp.stack([ret_gn_w.reshape(depth, RET_HEADS, RET_DV),
                        ret_gn_b.reshape(depth, RET_HEADS, RET_DV)], axis=2)

    xs = x.reshape(b * n, d)
    cx = ctx.reshape(b * n_ctx, d)
    mods, ln = mods_all, ln_all
    for l in range(depth):
        last = l == depth - 1
        xs = _ffn(xs, mods, lat_row, *ffa, ln, layer=l, tm=tm, mod_base=0, ln_row=0, alpha=alpha)
        cx = _ffn(cx, mods, ctx_row, *ffa, ln, layer=l, tm=n_ctx, mod_base=0, ln_row=0, alpha=alpha)

        px = _proj(xs, mods, lat_row, w_in, cos_x, sin_x, lat_rope, cs, layer=l, tm=tm,
                   grid_rows_out=True)
        pc = _proj(cx, mods, ctx_row, w_in, cos_c, sin_c, ctx_rope, cs, layer=l, tm=n_ctx,
                   grid_rows_out=False)
        q_x, k_x, v_x, g_x, nq_x, nk_x, nv_x = [a.reshape(b, n, -1) for a in px[:7]]
        gt_x = px[9]
        q_c, k_c, v_c, g_c, nq_c, nk_c, nv_c, zre_c, zim_c, gt_c = [a.reshape(b, n_ctx, -1) for a in pc]

        ret_x = _retention(lg_all, q_x, k_x, v_x, g_x, gn_all, k_c, v_c, layer=l)
        bias = _na_bias_tables(na_rpb[l])
        na_x = _neighbourhood_attention(nq_x, nk_x, nv_x, nk_c, nv_c, bias)
        a_re, a_im = _dft_rows(px[7].reshape(b, rows, -1), px[8].reshape(b, rows, -1), w2)
        fou_x = _dft_cols(a_re, a_im, twc, tws, kmat, wf, layer=l, scale=fou_scale_x)
        xs = _merge(xs, mods, lat_row, ret_x.reshape(b * n, -1), na_x.reshape(b * n, -1),
                    fou_x.reshape(b * n, d), gt_x, wr, wn, wo, ln,
                    layer=l, tm=tm, alpha=alpha)
        xs = _ffn(xs, mods, lat_row, *ffb, ln, layer=l, tm=tm, mod_base=6, ln_row=2, alpha=alpha)
        if not last:
            ret_c = _retention(lg_all, q_c, k_c, v_c, g_c, gn_all, layer=l)
            na_c = _context_attention(nq_c, nk_c, nv_c)
            fou_c = _ctx_fourier(zre_c, zim_c, cmat_ctx, wf, layer=l, scale=fou_scale_c)
            cx = _merge(cx, mods, ctx_row, ret_c.reshape(b * n_ctx, -1), na_c.reshape(b * n_ctx, -1),
                        fou_c.reshape(b * n_ctx, d), gt_c.reshape(b * n_ctx, -1), wr, wn, wo, ln,
                        layer=l, tm=n_ctx, alpha=alpha)
            cx = _ffn(cx, mods, ctx_row, *ffb, ln, layer=l, tm=n_ctx, mod_base=6, ln_row=2,
                      alpha=alpha)
    return xs.reshape(b, n, d)
```

```python
import functools
import math

import numpy as np
import jax
import jax.numpy as jnp
from jax import lax
from jax.experimental import pallas as pl
from jax.experimental.pallas import tpu as pltpu

F32 = jnp.float32
BF16 = jnp.bfloat16

GRID_W = 64
RET_HEADS = 4
RET_DK = 128
RET_DV = 256
NA_HEADS = 8
NA_DH = 64
NA_KH = 8
NA_KW = 16
FOU_GROUPS = 4
FOU_DG = 128
N_BRANCH = 3
ROPE_BASE = 10000.0
LN_EPS = 1e-6

V7X_VMEM_BYTES = 64 * 1024 * 1024
V7X_LANES = 128
V7X_SUBLANES = 8

TOKEN_TILE = 512
RET_CHUNK = 256
TOKEN_SPLIT = 2
SUBTILE_MIN = 256
RET_UNROLL = 8
NA_GROUP_ROWS = 4
NA_KEY_ROWS = NA_GROUP_ROWS + NA_KH
FOU_K1_BLOCK = V7X_SUBLANES
NEG_BIG = -1e30
LOG2_E = math.log2(math.e)


def _cparams(sem, vmem_mb):
    return pltpu.CompilerParams(dimension_semantics=sem,
                                vmem_limit_bytes=int(vmem_mb * 1024 * 1024))


def _resident(shape, layer=None):
    nd = len(shape)
    if layer is None:
        return pl.BlockSpec(shape, lambda *_: (0,) * nd, pipeline_mode=pl.Buffered(1))
    return pl.BlockSpec((None,) + tuple(shape), lambda *_: (layer,) + (0,) * nd,
                        pipeline_mode=pl.Buffered(1))


def _row_splits(tm):
    n = max(1, min(TOKEN_SPLIT, tm // SUBTILE_MIN))
    step = tm // n
    return [slice(s * step, (s + 1) * step) for s in range(n)]


def _ln_rows(x):
    mu = jnp.mean(x, axis=-1, keepdims=True)
    xc = x - mu
    var = jnp.mean(xc * xc, axis=-1, keepdims=True)
    return xc * lax.rsqrt(var + LN_EPS)


def _sigmoid(x):
    return 1.0 / (1.0 + jnp.exp(-x))


def _dot(a, b):
    return jnp.dot(a, b, preferred_element_type=F32)


def _dot_nt(a, b):
    return lax.dot_general(a, b, (((1,), (1,)), ((), ())), preferred_element_type=F32)


def _dot_tn(a, b):
    return lax.dot_general(a, b, (((0,), (0,)), ((), ())), preferred_element_type=F32)


def _mods_kernel(c_ref, w_ref, b_ref, o_ref):
    c = c_ref[...]
    h = c * _sigmoid(c)
    o_ref[0] = jnp.dot(h, w_ref[0], preferred_element_type=F32,
                       precision=lax.Precision.HIGHEST) + b_ref[0]


def _ada_mods(cvec, ada_w, ada_b):
    depth, d, nine_d = ada_w.shape
    nblk = nine_d // d
    return pl.pallas_call(
        _mods_kernel,
        grid=(depth, nblk),
        in_specs=[pl.BlockSpec((8, d), lambda l, j: (0, 0)),
                  pl.BlockSpec((1, d, d), lambda l, j: (l, 0, j)),
                  pl.BlockSpec((1, 1, d), lambda l, j: (l, 0, j))],
        out_specs=pl.BlockSpec((1, 8, d), lambda l, j: (l, 0, j)),
        out_shape=jax.ShapeDtypeStruct((depth, 8, nine_d), F32),
        compiler_params=_cparams(("parallel", "parallel"), 32),
        name="ada_mods",
    )(cvec, ada_w, ada_b.reshape(depth, 1, nine_d))


def _ffn_kernel(x_ref, mods_ref, w1_ref, w3_ref, w2_ref, ln_ref, o_ref, *, mod_base, ln_row, alpha):
    m = mods_ref[...]
    shift = m[mod_base:mod_base + 1]
    scale = m[mod_base + 1:mod_base + 2]
    gate = m[mod_base + 2:mod_base + 3]
    ln = ln_ref[...]
    for rows in _row_splits(x_ref.shape[0]):
        x = x_ref[rows, :]
        h = (_ln_rows(x) * (1.0 + scale) + shift).astype(BF16)
        a = _dot(h, w1_ref[...])
        b = _dot(h, w3_ref[...])
        g = (a * _sigmoid(a) * b).astype(BF16)
        y = _dot(g, w2_ref[...])
        z = alpha * x + 0.5 * gate * y
        o_ref[rows, :] = _ln_rows(z) * ln[ln_row:ln_row + 1] + ln[3 + ln_row:4 + ln_row]


def _ffn(x2, mods, mods_row, w1, w3, w2, ln, *, layer, tm, mod_base, ln_row, alpha):
    t, d = x2.shape
    dff = w1.shape[2]
    kern = functools.partial(_ffn_kernel, mod_base=mod_base, ln_row=ln_row, alpha=alpha)
    return pl.pallas_call(
        kern,
        grid=(t // tm,),
        in_specs=[pl.BlockSpec((tm, d), lambda i: (i, 0)),
                  pl.BlockSpec((None, None, 9, d), lambda i: (layer, mods_row(i), 0, 0)),
                  _resident((d, dff), layer), _resident((d, dff), layer), _resident((dff, d), layer),
                  _resident((6, d), layer)],
        out_specs=pl.BlockSpec((tm, d), lambda i: (i, 0)),
        out_shape=jax.ShapeDtypeStruct((t, d), F32),
        compiler_params=_cparams(("parallel",), 56),
        name="ffn",
    )(x2, mods, w1, w3, w2, ln)


def _rope(t, cos, sin_signed, first_half):
    swapped = jnp.where(first_half, pltpu.roll(t, 96, 1), pltpu.roll(t, 32, 1))
    return t * cos + swapped * sin_signed


def _proj_kernel(x_ref, mods_ref, w_ref, cos_ref, sin_ref, cs_ref, *rest,
                 q_scale, nq_scale, grid_rows_out):
    extra = rest[:-10]
    q_ref, k_ref, v_ref, g_ref, nq_ref, nk_ref, nv_ref, zre_ref, zim_ref, gt_ref = rest[-10:]
    x = x_ref[...]
    m = mods_ref[...]
    h = (_ln_rows(x) * (1.0 + m[4:5]) + m[3:4]).astype(BF16)
    tm = x.shape[0]
    cw = 512

    def seg(j):
        return _dot(h, w_ref[:, j * cw:(j + 1) * cw])

    cos = cos_ref[...]
    sin = sin_ref[...]
    lane = lax.broadcasted_iota(jnp.int32, (tm, V7X_LANES), 1)
    first_half = (lane & 32) == 0

    r = seg(0) * q_scale
    for hh in range(RET_HEADS):
        sl = slice(hh * RET_DK, (hh + 1) * RET_DK)
        q_ref[:, sl] = _rope(r[:, sl], cos, sin, first_half).astype(BF16)
    r = seg(1)
    for hh in range(RET_HEADS):
        sl = slice(hh * RET_DK, (hh + 1) * RET_DK)
        k_ref[:, sl] = _rope(r[:, sl], cos, sin, first_half).astype(BF16)
    for j in range(2):
        v_ref[:, j * cw:(j + 1) * cw] = seg(2 + j).astype(BF16)
    for j in range(2):
        r = seg(4 + j)
        g_ref[:, j * cw:(j + 1) * cw] = (r * _sigmoid(r)).astype(BF16)
    nq_ref[...] = (seg(6) * nq_scale).astype(BF16)
    nk_ref[...] = seg(7).astype(BF16)
    nv_ref[...] = seg(8).astype(BF16)
    u = seg(9).astype(BF16)
    cs = cs_ref[...]
    if grid_rows_out:
        perm_ref, = extra
        u = _dot(perm_ref[...], u).astype(BF16)
        n_r = tm // GRID_W
        for gi in range(FOU_GROUPS):
            sl = slice(gi * FOU_DG, (gi + 1) * FOU_DG)
            z = _dot(u[:, sl], cs)
            for c in range(GRID_W):
                dst = slice((c * FOU_GROUPS + gi) * FOU_DG, (c * FOU_GROUPS + gi + 1) * FOU_DG)
                zre_ref[:, dst] = z[c * n_r:(c + 1) * n_r, :FOU_DG]
                zim_ref[:, dst] = z[c * n_r:(c + 1) * n_r, FOU_DG:]
    else:
        for gi in range(FOU_GROUPS):
            sl = slice(gi * FOU_DG, (gi + 1) * FOU_DG)
            z = _dot(u[:, sl], cs)
            zre_ref[:, sl] = z[:, :FOU_DG].astype(BF16)
            zim_ref[:, sl] = z[:, FOU_DG:].astype(BF16)
    for j in range(6):
        gt_ref[:, j * cw:(j + 1) * cw] = _sigmoid(seg(10 + j)).astype(BF16)


def _proj(x2, mods, mods_row, w_in, cos, sin, rope_row, cs, *, layer, tm, grid_rows_out):
    t, d = x2.shape
    d_in = w_in.shape[2]
    ch = FOU_GROUPS * FOU_DG
    widths = (512, 512, 1024, 1024, 512, 512, 512, ch, ch, 3072)
    kern = functools.partial(_proj_kernel, q_scale=RET_DK ** -0.5, nq_scale=NA_DH ** -0.5 * LOG2_E,
                             grid_rows_out=grid_rows_out)
    out_specs = [pl.BlockSpec((tm, w), lambda i: (i, 0)) for w in widths]
    out_shape = [jax.ShapeDtypeStruct((t, w), BF16) for w in widths]
    in_specs = [pl.BlockSpec((tm, d), lambda i: (i, 0)),
                pl.BlockSpec((None, None, 9, d), lambda i: (layer, mods_row(i), 0, 0)),
                _resident((d, d_in), layer),
                pl.BlockSpec((tm, V7X_LANES), lambda i: (rope_row(i), 0)),
                pl.BlockSpec((tm, V7X_LANES), lambda i: (rope_row(i), 0)),
                _resident((FOU_DG, 2 * FOU_DG))]
    args = [x2, mods, w_in, cos, sin, cs]
    if grid_rows_out:
        n_r = tm // GRID_W
        for o in (7, 8):
            out_specs[o] = pl.BlockSpec((n_r, GRID_W * ch), lambda i: (i, 0))
            out_shape[o] = jax.ShapeDtypeStruct((t // GRID_W, GRID_W * ch), F32)
        src = (np.arange(n_r)[None, :] * GRID_W + np.arange(GRID_W)[:, None]).reshape(-1)
        perm = np.zeros((tm, tm), np.float32)
        perm[np.arange(tm), src] = 1.0
        in_specs.append(_resident((tm, tm)))
        args.append(jnp.asarray(perm).astype(BF16))
    return pl.pallas_call(
        kern,
        grid=(t // tm,),
        in_specs=in_specs,
        out_specs=out_specs,
        out_shape=out_shape,
        compiler_params=_cparams(("parallel",), 56),
        name="mixer_in_proj",
    )(*args)


def _log_sigmoid(x):
    return jnp.minimum(x, 0.0) - jnp.log(1.0 + jnp.exp(-jnp.abs(x)))


def _kv_update(k, v, lf, lb, idx):
    n = k.shape[0]
    kf = k.astype(F32)
    kk = jnp.concatenate([(kf * jnp.exp(lf * (n - 1.0 - idx))).astype(BF16),
                          (kf * jnp.exp(lb * idx)).astype(BF16)], axis=1)
    return _dot_tn(kk, v)


def _ret_kernel(*refs, n_chunks, chunk, with_prefix):
    if with_prefix:
        lg_ref, q_ref, k_ref, v_ref, g_ref, gn_ref, kc_ref, vc_ref, o_ref, u_ref, s_ref = refs
    else:
        lg_ref, q_ref, k_ref, v_ref, g_ref, gn_ref, o_ref, u_ref, s_ref = refs
    c = chunk
    dk, dv = RET_DK, RET_DV
    ls = _log_sigmoid(lg_ref[...])
    lf = ls[0, 0:1, :]
    lb = ls[1, 0:1, :]
    lf_v = jnp.concatenate([lf] * (dv // V7X_LANES), axis=1)
    lb_v = jnp.concatenate([lb] * (dv // V7X_LANES), axis=1)
    lf_c = jnp.concatenate([lf] * (c // V7X_LANES), axis=1)
    lb_c = jnp.concatenate([lb] * (c // V7X_LANES), axis=1)
    idx = lax.broadcasted_iota(jnp.int32, (c, V7X_LANES), 0).astype(F32)

    def upd(i, carry):
        r0 = pl.multiple_of(i * c, c)
        u_ref[i] = _kv_update(k_ref[pl.ds(r0, c), :], v_ref[pl.ds(r0, c), :], lf, lb, idx)
        return carry
    lax.fori_loop(0, n_chunks, upd, 0, unroll=min(RET_UNROLL, n_chunks))

    if with_prefix:
        n_ctx = kc_ref.shape[0]
        idx_c = lax.broadcasted_iota(jnp.int32, (n_ctx, V7X_LANES), 0).astype(F32)
        u0 = _kv_update(kc_ref[...], vc_ref[...], lf, lb, idx_c)
        sf0, sb0 = u0[:dk], u0[dk:]
    else:
        sf0 = jnp.zeros((dk, dv), F32)
        sb0 = sf0
    gf_c = jnp.exp(lf_v * float(c))
    gb_c = jnp.exp(lb_v * float(c))

    def scan_f(i, sf):
        s_ref[i, :dk, :] = sf.astype(BF16)
        return sf * gf_c + u_ref[i, :dk, :]
    lax.fori_loop(0, n_chunks, scan_f, sf0)

    def scan_b(t, sb):
        i = n_chunks - 1 - t
        s_ref[i, dk:, :] = sb.astype(BF16)
        return sb * gb_c + u_ref[i, dk:, :]
    lax.fori_loop(0, n_chunks, scan_b, sb0)

    pr = (lax.broadcasted_iota(jnp.int32, (c, c), 0)
          - lax.broadcasted_iota(jnp.int32, (c, c), 1)).astype(F32)
    dmat = jnp.where(pr >= 0.0, jnp.exp(lf_c * jnp.maximum(pr, 0.0)),
                     jnp.exp(lb_c * jnp.maximum(-pr, 0.0)))
    qdf = jnp.exp(lf * (idx + 1.0))
    qdb = jnp.exp(lb * (float(c) - idx))
    gn = gn_ref[...]
    gn_w, gn_b = gn[0:1], gn[1:2]

    def out(i, carry):
        r0 = pl.multiple_of(i * c, c)
        q = q_ref[pl.ds(r0, c), :]
        k = k_ref[pl.ds(r0, c), :]
        v = v_ref[pl.ds(r0, c), :]
        inner = (_dot_nt(q, k) * dmat).astype(BF16)
        qf = q.astype(F32)
        qq = jnp.concatenate([(qf * qdf).astype(BF16), (qf * qdb).astype(BF16)], axis=1)
        o = _dot(inner, v) + _dot(qq, s_ref[i])
        o = _ln_rows(o) * gn_w + gn_b
        o_ref[pl.ds(r0, c), :] = (o * g_ref[pl.ds(r0, c), :].astype(F32)).astype(BF16)
        return carry
    lax.fori_loop(0, n_chunks, out, 0, unroll=min(RET_UNROLL, n_chunks))


def _retention(lg, q, k, v, g, gn, kc=None, vc=None, *, layer):
    b, n, _ = q.shape
    c = min(RET_CHUNK, n)
    nc = n // c
    with_prefix = kc is not None
    kern = functools.partial(_ret_kernel, n_chunks=nc, chunk=c, with_prefix=with_prefix)
    in_specs = [pl.BlockSpec((None, None, 2, 8, V7X_LANES), lambda bi, hi: (layer, hi, 0, 0, 0)),
                pl.BlockSpec((None, n, RET_DK), lambda bi, hi: (bi, 0, hi)),
                pl.BlockSpec((None, n, RET_DK), lambda bi, hi: (bi, 0, hi)),
                pl.BlockSpec((None, n, RET_DV), lambda bi, hi: (bi, 0, hi)),
                pl.BlockSpec((None, n, RET_DV), lambda bi, hi: (bi, 0, hi)),
                pl.BlockSpec((None, None, 2, RET_DV), lambda bi, hi: (layer, hi, 0, 0))]
    args = [lg, q, k, v, g, gn]
    if with_prefix:
        n_ctx = kc.shape[1]
        in_specs += [pl.BlockSpec((None, n_ctx, RET_DK), lambda bi, hi: (bi, 0, hi)),
                     pl.BlockSpec((None, n_ctx, RET_DV), lambda bi, hi: (bi, 0, hi))]
        args += [kc, vc]
    return pl.pallas_call(
        kern,
        grid=(b, RET_HEADS),
        in_specs=in_specs,
        out_specs=pl.BlockSpec((None, n, RET_DV), lambda bi, hi: (bi, 0, hi)),
        out_shape=jax.ShapeDtypeStruct((b, n, RET_HEADS * RET_DV), BF16),
        scratch_shapes=[pltpu.VMEM((nc, 2 * RET_DK, RET_DV), F32),
                        pltpu.VMEM((nc, 2 * RET_DK, RET_DV), BF16)],
        compiler_params=_cparams(("parallel", "parallel"), 56),
        name="retention",
    )(*args)


def _softmax_pv(s_parts, v_parts):
    m = None
    for s in s_parts:
        mi = jnp.max(s, axis=-1, keepdims=True)
        m = mi if m is None else jnp.maximum(m, mi)
    l = None
    o = None
    for s, v in zip(s_parts, v_parts):
        p = jnp.exp2(s - m)
        li = jnp.sum(p, axis=-1, keepdims=True)
        oi = _dot(p.astype(BF16), v)
        l = li if l is None else l + li
        o = oi if o is None else o + oi
    return o / l


def _na_kernel(q_ref, k_ref, v_ref, kc_ref, vc_ref, bias_ref, o_ref, s_ref, *, n_groups, key_row0_max):
    gq = NA_GROUP_ROWS * GRID_W
    nk = NA_KEY_ROWS * GRID_W
    lane = lax.broadcasted_iota(jnp.int32, (gq, V7X_LANES), 1)
    head0 = lane < NA_DH
    kc = kc_ref[...]
    vc = vc_ref[...]

    def key_start(g):
        kb = jnp.clip(g * NA_GROUP_ROWS - NA_KH // 2, 0, key_row0_max)
        return pl.multiple_of(kb * GRID_W, GRID_W)

    def scores(g, slot):
        q0 = pl.multiple_of(g * gq, gq)
        cfg = jnp.where(g == 0, 0, jnp.where(g == n_groups - 1, 2, 1))
        q = q_ref[pl.ds(q0, gq), :]
        kl = k_ref[pl.ds(key_start(g), nk), :]
        zero = jnp.zeros_like(q)
        for hh in range(2):
            qh = jnp.where(head0, q, zero) if hh == 0 else jnp.where(head0, zero, q)
            s_ref[slot, hh, :, :nk] = _dot_nt(qh, kl) + bias_ref[cfg, hh]
            s_ref[slot, hh, :, nk:] = _dot_nt(qh, kc)

    def finish(g, slot):
        q0 = pl.multiple_of(g * gq, gq)
        vl = v_ref[pl.ds(key_start(g), nk), :]
        outs = [_softmax_pv([s_ref[slot, hh, :, :nk], s_ref[slot, hh, :, nk:]], [vl, vc])
                for hh in range(2)]
        o_ref[pl.ds(q0, gq), :] = jnp.where(head0, outs[0], outs[1]).astype(BF16)

    scores(0, 0)

    def pair(j, carry):
        g = 2 * j
        scores(g + 1, 1)
        finish(g, 0)
        scores(jnp.minimum(g + 2, n_groups - 1), 0)
        finish(g + 1, 1)
        return carry
    lax.fori_loop(0, n_groups // 2, pair, 0)


def _na_bias_tables(rpb):
    h = rpb.shape[0]
    w = GRID_W
    cols = np.arange(w)
    cs = np.clip(cols - NA_KW // 2, 0, w - NA_KW)
    kcol = np.arange(w)[None, :]
    colmask = (kcol >= cs[:, None]) & (kcol < cs[:, None] + NA_KW)
    pad = w - 1
    rpb_p = jnp.pad(rpb, ((0, 0), (0, 0), (pad, pad)))
    toep = jnp.stack([rpb_p[:, :, pad + NA_KW - 1 - c: pad + NA_KW - 1 - c + w] for c in range(w)],
                     axis=2)
    toep = jnp.where(colmask[None, None], toep * LOG2_E, NEG_BIG)
    n_dr = toep.shape[1]
    toep = jnp.concatenate([toep, jnp.full((h, 1, w, w), NEG_BIG, F32)], axis=1)
    dr = np.full((3, NA_GROUP_ROWS, NA_KEY_ROWS), n_dr, np.int32)
    for cfg in range(3):
        rel = NA_GROUP_ROWS * cfg
        for rr in range(NA_GROUP_ROWS):
            lo = (0, rr, NA_KEY_ROWS - NA_KH)[cfg]
            for i in range(lo, lo + NA_KH):
                dr[cfg, rr, i] = i - rel - rr + NA_KH - 1
    blocks = jnp.take(toep, jnp.asarray(dr.reshape(-1)), axis=1)
    blocks = blocks.reshape(h, 3, NA_GROUP_ROWS, NA_KEY_ROWS, w, w)
    return jnp.transpose(blocks, (1, 0, 2, 4, 3, 5)).reshape(
        3, h, NA_GROUP_ROWS * w, NA_KEY_ROWS * w)


def _neighbourhood_attention(nq, nk, nv, kc, vc, bias):
    b, n, hw = nq.shape
    rows = n // GRID_W
    n_ctx = kc.shape[1]
    n_groups = rows // NA_GROUP_ROWS
    pairs = NA_HEADS // 2
    gq = NA_GROUP_ROWS * GRID_W
    nkeys = NA_KEY_ROWS * GRID_W
    kern = functools.partial(_na_kernel, n_groups=n_groups, key_row0_max=rows - NA_KEY_ROWS)
    tok = pl.BlockSpec((None, n, V7X_LANES), lambda bi, pi: (bi, 0, pi))
    ctx = pl.BlockSpec((None, n_ctx, V7X_LANES), lambda bi, pi: (bi, 0, pi))
    return pl.pallas_call(
        kern,
        grid=(b, pairs),
        in_specs=[tok, tok, tok, ctx, ctx,
                  pl.BlockSpec((3, 2, gq, nkeys), lambda bi, pi: (0, pi, 0, 0))],
        out_specs=tok,
        out_shape=jax.ShapeDtypeStruct((b, n, hw), BF16),
        scratch_shapes=[pltpu.VMEM((2, 2, gq, nkeys + n_ctx), F32)],
        compiler_params=_cparams(("parallel", "parallel"), 48),
        name="neighbourhood_attention",
    )(nq, nk, nv, kc, vc, bias)


def _ctx_attn_kernel(q_ref, k_ref, v_ref, o_ref):
    q = q_ref[...]
    k = k_ref[...]
    v = v_ref[...]
    lane = lax.broadcasted_iota(jnp.int32, q.shape, 1)
    head0 = lane < NA_DH
    zero = jnp.zeros_like(q)
    outs = []
    for hh in range(2):
        qh = jnp.where(head0, q, zero) if hh == 0 else jnp.where(head0, zero, q)
        outs.append(_softmax_pv([_dot_nt(qh, k)], [v]))
    o_ref[...] = jnp.where(head0, outs[0], outs[1]).astype(BF16)


def _context_attention(q, k, v):
    b, n_ctx, hw = q.shape
    blk = pl.BlockSpec((None, n_ctx, V7X_LANES), lambda bi, pi: (bi, 0, pi))
    return pl.pallas_call(
        _ctx_attn_kernel,
        grid=(b, NA_HEADS // 2),
        in_specs=[blk, blk, blk],
        out_specs=blk,
        out_shape=jax.ShapeDtypeStruct((b, n_ctx, hw), BF16),
        compiler_params=_cparams(("parallel", "parallel"), 32),
        name="context_attention",
    )(q, k, v)


def _dft_rows_kernel(zre_ref, zim_ref, w_ref, are_ref, aim_ref):
    r = zre_ref.shape[0]
    z = jnp.concatenate([zre_ref[...].astype(BF16), zim_ref[...].astype(BF16)], axis=0)
    a = _dot(w_ref[...], z)
    are_ref[...] = a[:r]
    aim_ref[...] = a[r:]


def _dft_rows(zre, zim, w2):
    b, r, wc = zre.shape
    cb = min(4096, wc)
    blk = pl.BlockSpec((None, r, cb), lambda bi, ci: (bi, 0, ci))
    return pl.pallas_call(
        _dft_rows_kernel,
        grid=(b, wc // cb),
        in_specs=[blk, blk, _resident((2 * r, 2 * r))],
        out_specs=[blk, blk],
        out_shape=[jax.ShapeDtypeStruct((b, r, wc), F32)] * 2,
        compiler_params=_cparams(("parallel", "parallel"), 40),
        name="dft_rows",
    )(zre, zim, w2)


def _dft_cols_kernel(are_ref, aim_ref, twc_ref, tws_ref, km_ref, wo_ref, o_ref, *, scale, ch):
    w = are_ref.shape[1] // ch
    ar = jnp.concatenate([are_ref[:, c * ch:(c + 1) * ch] for c in range(w)], axis=0)
    ai = jnp.concatenate([aim_ref[:, c * ch:(c + 1) * ch] for c in range(w)], axis=0)
    reps = ch // V7X_LANES
    tc = jnp.concatenate([twc_ref[...]] * reps, axis=1)
    ts = jnp.concatenate([tws_ref[...]] * reps, axis=1)
    a = jnp.concatenate([(ar * tc - ai * ts).astype(BF16), (ar * ts + ai * tc).astype(BF16)], axis=0)
    y = _dot(km_ref[...], a) * scale
    out = _dot(y.astype(BF16), wo_ref[...])
    o_ref[...] = out.reshape(o_ref.shape)


def _dft_cols(a_re, a_im, twc, tws, kmat, w_out, *, layer, scale):
    b, r, wc = a_re.shape
    w = GRID_W
    ch = wc // w
    d = w_out.shape[2]
    jb = FOU_K1_BLOCK
    blk = pl.BlockSpec((None, jb, wc), lambda bi, ji: (bi, ji, 0))
    tw = pl.BlockSpec((jb * w, V7X_LANES), lambda bi, ji: (ji, 0))
    kern = functools.partial(_dft_cols_kernel, scale=scale, ch=ch)
    return pl.pallas_call(
        kern,
        grid=(b, r // jb),
        in_specs=[blk, blk, tw, tw, _resident(kmat.shape), _resident((ch, d), layer)],
        out_specs=pl.BlockSpec((None, w, jb, d), lambda bi, ji: (bi, 0, ji, 0)),
        out_shape=jax.ShapeDtypeStruct((b, w, r, d), F32),
        compiler_params=_cparams(("parallel", "parallel"), 32),
        name="dft_cols",
    )(a_re, a_im, twc, tws, kmat, w_out)


def _ctx_fourier_kernel(zre_ref, zim_ref, cm_ref, wo_ref, o_ref, *, scale):
    z = jnp.concatenate([zre_ref[...], zim_ref[...]], axis=0)
    y = _dot(cm_ref[...], z) * scale
    o_ref[...] = _dot(y.astype(BF16), wo_ref[...])


def _ctx_fourier(zre, zim, cmat, w_out, *, layer, scale):
    b, n_ctx, ch = zre.shape
    d = w_out.shape[2]
    blk = pl.BlockSpec((None, n_ctx, ch), lambda bi: (bi, 0, 0))
    kern = functools.partial(_ctx_fourier_kernel, scale=scale)
    return pl.pallas_call(
        kern,
        grid=(b,),
        in_specs=[blk, blk, _resident(cmat.shape), _resident((ch, d), layer)],
        out_specs=pl.BlockSpec((None, n_ctx, d), lambda bi: (bi, 0, 0)),
        out_shape=jax.ShapeDtypeStruct((b, n_ctx, d), F32),
        compiler_params=_cparams(("parallel",), 32),
        name="ctx_fourier",
    )(zre, zim, cmat, w_out)


def _merge_kernel(x_ref, mods_ref, ret_ref, na_ref, fou_ref, gt_ref, wr_ref, wn_ref, wo_ref, ln_ref,
                  o_ref, *, alpha):
    d = x_ref.shape[1]
    y_ret = _dot(ret_ref[...], wr_ref[...])
    y_na = _dot(na_ref[...], wn_ref[...])
    mix = (gt_ref[:, 0:d].astype(F32) * y_ret + gt_ref[:, d:2 * d].astype(F32) * y_na
           + gt_ref[:, 2 * d:3 * d].astype(F32) * fou_ref[...])
    y = _dot(mix.astype(BF16), wo_ref[...])
    m = mods_ref[...]
    z = alpha * x_ref[...] + m[5:6] * y
    ln = ln_ref[...]
    o_ref[...] = _ln_rows(z) * ln[1:2] + ln[4:5]


def _merge(x2, mods, mods_row, ret, na, fou, gt, wr, wn, wo, ln, *, layer, tm, alpha):
    t, d = x2.shape
    kern = functools.partial(_merge_kernel, alpha=alpha)

    def rows(wd):
        return pl.BlockSpec((tm, wd), lambda i: (i, 0))
    return pl.pallas_call(
        kern,
        grid=(t // tm,),
        in_specs=[rows(d), pl.BlockSpec((None, None, 9, d), lambda i: (layer, mods_row(i), 0, 0)),
                  rows(ret.shape[1]), rows(na.shape[1]), rows(d), rows(gt.shape[1]),
                  _resident(wr.shape[1:], layer), _resident(wn.shape[1:], layer),
                  _resident(wo.shape[1:], layer), _resident((6, d), layer)],
        out_specs=rows(d),
        out_shape=jax.ShapeDtypeStruct((t, d), F32),
        compiler_params=_cparams(("parallel",), 48),
        name="merge",
    )(x2, mods, ret, na, fou, gt, wr, wn, wo, ln)


def _rope_tables(n):
    half = RET_DK // 4
    t = np.arange(n)
    row, col = t // GRID_W, t % GRID_W
    freqs = ROPE_BASE ** (-np.arange(half, dtype=np.float64) / half)
    ang_r = row.astype(np.float64)[:, None] * freqs[None, :]
    ang_c = col.astype(np.float64)[:, None] * freqs[None, :]
    cos = np.concatenate([np.cos(ang_r)] * 2 + [np.cos(ang_c)] * 2, axis=1)
    sin = np.concatenate([-np.sin(ang_r), np.sin(ang_r), -np.sin(ang_c), np.sin(ang_c)], axis=1)
    return jnp.asarray(cos, F32), jnp.asarray(sin, F32)


def _dft_tables(n):
    w = GRID_W
    r = n // w
    jb = FOU_K1_BLOCK
    two_pi = 2.0 * np.pi
    ch = np.arange(FOU_DG)
    ang = two_pi * ((ch[:, None] * ch[None, :]) % FOU_DG) / FOU_DG
    cs = np.concatenate([np.cos(ang), np.sin(ang)], axis=1)
    k1 = np.arange(r)
    ang = two_pi * ((k1[:, None] * k1[None, :]) % r) / r
    w2 = np.block([[np.cos(ang), -np.sin(ang)], [np.sin(ang), np.cos(ang)]])
    c = np.arange(w)
    k1_blocks = k1.reshape(r // jb, 1, jb)
    ang = two_pi * (k1_blocks * c[None, :, None]).reshape(-1) / n
    twc = np.repeat(np.cos(ang)[:, None], V7X_LANES, axis=1)
    tws = np.repeat(np.sin(ang)[:, None], V7X_LANES, axis=1)
    ang = two_pi * ((c[:, None] * c[None, :]) % w) / w
    eye = np.eye(jb)
    kc = np.einsum("kc,ji->kjci", np.cos(ang), eye).reshape(w * jb, w * jb)
    ks = np.einsum("kc,ji->kjci", np.sin(ang), eye).reshape(w * jb, w * jb)
    kmat = np.concatenate([kc, -ks], axis=1)
    as_bf16 = lambda a: jnp.asarray(a, F32).astype(BF16)
    return (as_bf16(cs), as_bf16(w2), jnp.asarray(twc, F32), jnp.asarray(tws, F32), as_bf16(kmat))


def _ctx_dft_matrix(n_ctx):
    m = np.arange(n_ctx)
    ang = 2.0 * np.pi * ((m[:, None] * m[None, :]) % n_ctx) / n_ctx
    return jnp.asarray(np.concatenate([np.cos(ang), -np.sin(ang)], axis=1), F32).astype(BF16)


def kernel(x, c, ctx, c_ctx, ada_w, ada_b, ln_w, ln_b, ffa_w1, ffa_w3, ffa_w2, mix_w_in,
           ret_decay_logit, ret_gn_w, ret_gn_b, ret_w_out, na_rpb, na_w_out, fou_w_out, mix_w_o,
           ffb_w1, ffb_w3, ffb_w2):
    b, n, d = x.shape
    n_ctx = ctx.shape[1]
    depth = ada_w.shape[0]
    rows = n // GRID_W
    assert n % TOKEN_TILE == 0 and rows % (2 * NA_GROUP_ROWS) == 0 and rows >= NA_KEY_ROWS
    assert rows % FOU_K1_BLOCK == 0 and b + 1 <= 8
    alpha = (2 * depth) ** 0.25
    tm = TOKEN_TILE
    tiles_per_seq = n // tm
    lat_row = lambda i: i // tiles_per_seq
    ctx_row = lambda i: b
    lat_rope = lambda i: i % tiles_per_seq
    ctx_rope = lambda i: 0

    cos_x, sin_x = _rope_tables(n)
    cos_c = jnp.ones((n_ctx, V7X_LANES), F32)
    sin_c = jnp.zeros((n_ctx, V7X_LANES), F32)
    cs, w2, twc, tws, kmat = _dft_tables(n)
    cmat_ctx = _ctx_dft_matrix(n_ctx)
    fou_scale_x = 1.0 / math.sqrt(n * FOU_DG)
    fou_scale_c = 1.0 / math.sqrt(n_ctx * FOU_DG)

    cvec = jnp.concatenate([c, c_ctx[None, :], jnp.zeros((8 - b - 1, d), F32)], axis=0)
    mods_all = _ada_mods(cvec, ada_w, ada_b).reshape(depth, 8, 9, d)
    ln_all = jnp.concatenate([ln_w, ln_b], axis=1)
    to_bf16 = lambda a: a.astype(BF16)
    ffa = [to_bf16(a) for a in (ffa_w1, ffa_w3, ffa_w2)]
    ffb = [to_bf16(a) for a in (ffb_w1, ffb_w3, ffb_w2)]
    w_in, wr, wn, wf, wo = [to_bf16(a) for a in (mix_w_in, ret_w_out, na_w_out, fou_w_out, mix_w_o)]
    lg_all = jnp.broadcast_to(
        jnp.transpose(ret_decay_logit, (0, 2, 1))[:, :, :, None, None],
        (depth, RET_HEADS, 2, 8, V7X_LANES)).astype(F32)
    gn_all = jnp.stack([ret_gn_w.reshape(depth, RET_HEADS, RET_DV),
                        ret_gn_b.reshape(depth, RET_HEADS, RET_DV)], axis=2)

    xs = x.reshape(b * n, d)
    cx = ctx.reshape(b * n_ctx, d)
    mods, ln = mods_all, ln_all
    for l in range(depth):
        last = l == depth - 1
        xs = _ffn(xs, mods, lat_row, *ffa, ln, layer=l, tm=tm, mod_base=0, ln_row=0, alpha=alpha)
        cx = _ffn(cx, mods, ctx_row, *ffa, ln, layer=l, tm=n_ctx, mod_base=0, ln_row=0, alpha=alpha)

        px = _proj(xs, mods, lat_row, w_in, cos_x, sin_x, lat_rope, cs, layer=l, tm=tm,
                   grid_rows_out=True)
        pc = _proj(cx, mods, ctx_row, w_in, cos_c, sin_c, ctx_rope, cs, layer=l, tm=n_ctx,
                   grid_rows_out=False)
        q_x, k_x, v_x, g_x, nq_x, nk_x, nv_x = [a.reshape(b, n, -1) for a in px[:7]]
        gt_x = px[9]
        q_c, k_c, v_c, g_c, nq_c, nk_c, nv_c, zre_c, zim_c, gt_c = [a.reshape(b, n_ctx, -1) for a in pc]

        ret_x = _retention(lg_all, q_x, k_x, v_x, g_x, gn_all, k_c, v_c, layer=l)
        bias = _na_bias_tables(na_rpb[l])
        na_x = _neighbourhood_attention(nq_x, nk_x, nv_x, nk_c, nv_c, bias)
        a_re, a_im = _dft_rows(px[7].reshape(b, rows, -1), px[8].reshape(b, rows, -1), w2)
        fou_x = _dft_cols(a_re, a_im, twc, tws, kmat, wf, layer=l, scale=fou_scale_x)
        xs = _merge(xs, mods, lat_row, ret_x.reshape(b * n, -1), na_x.reshape(b * n, -1),
                    fou_x.reshape(b * n, d), gt_x, wr, wn, wo, ln,
                    layer=l, tm=tm, alpha=alpha)
        xs = _ffn(xs, mods, lat_row, *ffb, ln, layer=l, tm=tm, mod_base=6, ln_row=2, alpha=alpha)
        if not last:
            ret_c = _retention(lg_all, q_c, k_c, v_c, g_c, gn_all, layer=l)
            na_c = _context_attention(nq_c, nk_c, nv_c)
            fou_c = _ctx_fourier(zre_c, zim_c, cmat_ctx, wf, layer=l, scale=fou_scale_c)
            cx = _merge(cx, mods, ctx_row, ret_c.reshape(b * n_ctx, -1), na_c.reshape(b * n_ctx, -1),
                        fou_c.reshape(b * n_ctx, d), gt_c.reshape(b * n_ctx, -1), wr, wn, wo, ln,
                        layer=l, tm=n_ctx, alpha=alpha)
            cx = _ffn(cx, mods, ctx_row, *ffb, ln, layer=l, tm=n_ctx, mod_base=6, ln_row=2,
                      alpha=alpha)
    return xs.reshape(b, n, d)
```

```python
import functools
import math

import numpy as np
import jax
import jax.numpy as jnp
from jax import lax
from jax.experimental import pallas as pl
from jax.experimental.pallas import tpu as pltpu

F32 = jnp.float32
BF16 = jnp.bfloat16

GRID_W = 64
RET_HEADS = 4
RET_DK = 128
RET_DV = 256
NA_HEADS = 8
NA_DH = 64
NA_KH = 8
NA_KW = 16
FOU_GROUPS = 4
FOU_DG = 128
N_BRANCH = 3
ROPE_BASE = 10000.0
LN_EPS = 1e-6

V7X_VMEM_BYTES = 64 * 1024 * 1024
V7X_LANES = 128
V7X_SUBLANES = 8

TOKEN_TILE = 512
RET_CHUNK = 256
FFN_TILE = 1024
TOKEN_SPLIT = 4
SUBTILE_MIN = 256
RET_UNROLL = 8
NA_GROUP_ROWS = 4
NA_KEY_ROWS = NA_GROUP_ROWS + NA_KH
FOU_K1_BLOCK = V7X_SUBLANES
NEG_BIG = -1e30
LOG2_E = math.log2(math.e)


def _cparams(sem, vmem_mb):
    return pltpu.CompilerParams(dimension_semantics=sem,
                                vmem_limit_bytes=int(vmem_mb * 1024 * 1024))


def _resident(shape, layer=None):
    nd = len(shape)
    if layer is None:
        return pl.BlockSpec(shape, lambda *_: (0,) * nd, pipeline_mode=pl.Buffered(1))
    return pl.BlockSpec((None,) + tuple(shape), lambda *_: (layer,) + (0,) * nd,
                        pipeline_mode=pl.Buffered(1))


def _row_splits(tm):
    n = max(1, min(TOKEN_SPLIT, tm // SUBTILE_MIN))
    step = tm // n
    return [slice(s * step, (s + 1) * step) for s in range(n)]


def _ln_rows(x):
    mu = jnp.mean(x, axis=-1, keepdims=True)
    xc = x - mu
    var = jnp.mean(xc * xc, axis=-1, keepdims=True)
    return xc * lax.rsqrt(var + LN_EPS)


def _sigmoid(x):
    return 1.0 / (1.0 + jnp.exp(-x))


def _dot(a, b):
    return jnp.dot(a, b, preferred_element_type=F32)


def _dot_nt(a, b):
    return lax.dot_general(a, b, (((1,), (1,)), ((), ())), preferred_element_type=F32)


def _dot_tn(a, b):
    return lax.dot_general(a, b, (((0,), (0,)), ((), ())), preferred_element_type=F32)


def _mods_kernel(c_ref, w_ref, b_ref, o_ref):
    c = c_ref[...]
    h = (c * _sigmoid(c)).astype(BF16)
    o_ref[0] = _dot(h, w_ref[0].astype(BF16)) + b_ref[0]


def _ada_mods(cvec, ada_w, ada_b):
    depth, d, nine_d = ada_w.shape
    nblk = nine_d // d
    return pl.pallas_call(
        _mods_kernel,
        grid=(depth, nblk),
        in_specs=[pl.BlockSpec((8, d), lambda l, j: (0, 0)),
                  pl.BlockSpec((1, d, d), lambda l, j: (l, 0, j)),
                  pl.BlockSpec((1, 1, d), lambda l, j: (l, 0, j))],
        out_specs=pl.BlockSpec((1, 8, d), lambda l, j: (l, 0, j)),
        out_shape=jax.ShapeDtypeStruct((depth, 8, nine_d), F32),
        compiler_params=_cparams(("parallel", "parallel"), 32),
        name="ada_mods",
    )(cvec, ada_w, ada_b.reshape(depth, 1, nine_d))


def _ffn_kernel(x_ref, mods_ref, w1_ref, w3_ref, w2_ref, ln_ref, o_ref, *, mod_base, ln_row, alpha):
    m = mods_ref[...]
    shift = m[mod_base:mod_base + 1]
    scale = m[mod_base + 1:mod_base + 2]
    gate = m[mod_base + 2:mod_base + 3]
    ln = ln_ref[...]
    for rows in _row_splits(x_ref.shape[0]):
        x = x_ref[rows, :]
        h = (_ln_rows(x) * (1.0 + scale) + shift).astype(BF16)
        a = _dot(h, w1_ref[...])
        b = _dot(h, w3_ref[...])
        g = (a * _sigmoid(a) * b).astype(BF16)
        y = _dot(g, w2_ref[...])
        z = alpha * x + 0.5 * gate * y
        o_ref[rows, :] = _ln_rows(z) * ln[ln_row:ln_row + 1] + ln[3 + ln_row:4 + ln_row]


def _ffn(x2, mods, mods_row, w1, w3, w2, ln, *, layer, tm, mod_base, ln_row, alpha):
    t, d = x2.shape
    dff = w1.shape[2]
    kern = functools.partial(_ffn_kernel, mod_base=mod_base, ln_row=ln_row, alpha=alpha)
    return pl.pallas_call(
        kern,
        grid=(t // tm,),
        in_specs=[pl.BlockSpec((tm, d), lambda i: (i, 0)),
                  pl.BlockSpec((None, None, 9, d), lambda i: (layer, mods_row(i), 0, 0)),
                  _resident((d, dff), layer), _resident((d, dff), layer), _resident((dff, d), layer),
                  _resident((6, d), layer)],
        out_specs=pl.BlockSpec((tm, d), lambda i: (i, 0)),
        out_shape=jax.ShapeDtypeStruct((t, d), F32),
        compiler_params=_cparams(("parallel",), 62 if tm > TOKEN_TILE else 56),
        name="ffn",
    )(x2, mods, w1, w3, w2, ln)


def _rope(t, cos, sin_signed, first_half):
    swapped = jnp.where(first_half, pltpu.roll(t, 96, 1), pltpu.roll(t, 32, 1))
    return t * cos + swapped * sin_signed


def _proj_kernel(x_ref, mods_ref, w_ref, cos_ref, sin_ref, cs_ref, *rest,
                 q_scale, nq_scale, grid_rows_out):
    extra = rest[:-10]
    q_ref, k_ref, v_ref, g_ref, nq_ref, nk_ref, nv_ref, zre_ref, zim_ref, gt_ref = rest[-10:]
    x = x_ref[...]
    m = mods_ref[...]
    h = (_ln_rows(x) * (1.0 + m[4:5]) + m[3:4]).astype(BF16)
    tm = x.shape[0]
    cw = 512

    def seg(j):
        return _dot(h, w_ref[:, j * cw:(j + 1) * cw])

    cos = cos_ref[...]
    sin = sin_ref[...]
    lane = lax.broadcasted_iota(jnp.int32, (tm, V7X_LANES), 1)
    first_half = (lane & 32) == 0

    r = seg(0) * q_scale
    for hh in range(RET_HEADS):
        sl = slice(hh * RET_DK, (hh + 1) * RET_DK)
        q_ref[:, sl] = _rope(r[:, sl], cos, sin, first_half).astype(BF16)
    r = seg(1)
    for hh in range(RET_HEADS):
        sl = slice(hh * RET_DK, (hh + 1) * RET_DK)
        k_ref[:, sl] = _rope(r[:, sl], cos, sin, first_half).astype(BF16)
    for j in range(2):
        r = seg(4 + j)
        g_ref[:, j * cw:(j + 1) * cw] = (r * _sigmoid(r)).astype(BF16)
    for j in range(6):
        gt_ref[:, j * cw:(j + 1) * cw] = _sigmoid(seg(10 + j)).astype(BF16)
    u = seg(9).astype(BF16)
    cs = cs_ref[...]
    if grid_rows_out:
        perm_ref, = extra
        u = _dot(perm_ref[...], u).astype(BF16)
        n_r = tm // GRID_W
        for gi in range(FOU_GROUPS):
            sl = slice(gi * FOU_DG, (gi + 1) * FOU_DG)
            z = _dot(u[:, sl], cs)
            for c in range(GRID_W):
                dst = slice((c * FOU_GROUPS + gi) * FOU_DG, (c * FOU_GROUPS + gi + 1) * FOU_DG)
                zre_ref[:, dst] = z[c * n_r:(c + 1) * n_r, :FOU_DG]
                zim_ref[:, dst] = z[c * n_r:(c + 1) * n_r, FOU_DG:]
    else:
        for gi in range(FOU_GROUPS):
            sl = slice(gi * FOU_DG, (gi + 1) * FOU_DG)
            z = _dot(u[:, sl], cs)
            zre_ref[:, sl] = z[:, :FOU_DG].astype(BF16)
            zim_ref[:, sl] = z[:, FOU_DG:].astype(BF16)
    nq_ref[...] = (seg(6) * nq_scale).astype(BF16)
    for j in range(2):
        v_ref[:, j * cw:(j + 1) * cw] = seg(2 + j).astype(BF16)
    nk_ref[...] = seg(7).astype(BF16)
    nv_ref[...] = seg(8).astype(BF16)


def _proj(x2, mods, mods_row, w_in, cos, sin, rope_row, cs, *, layer, tm, grid_rows_out):
    t, d = x2.shape
    d_in = w_in.shape[2]
    ch = FOU_GROUPS * FOU_DG
    widths = (512, 512, 1024, 1024, 512, 512, 512, ch, ch, 3072)
    kern = functools.partial(_proj_kernel, q_scale=RET_DK ** -0.5, nq_scale=NA_DH ** -0.5 * LOG2_E,
                             grid_rows_out=grid_rows_out)
    out_specs = [pl.BlockSpec((tm, w), lambda i: (i, 0)) for w in widths]
    out_shape = [jax.ShapeDtypeStruct((t, w), BF16) for w in widths]
    in_specs = [pl.BlockSpec((tm, d), lambda i: (i, 0)),
                pl.BlockSpec((None, None, 9, d), lambda i: (layer, mods_row(i), 0, 0)),
                _resident((d, d_in), layer),
                pl.BlockSpec((tm, V7X_LANES), lambda i: (rope_row(i), 0)),
                pl.BlockSpec((tm, V7X_LANES), lambda i: (rope_row(i), 0)),
                _resident((FOU_DG, 2 * FOU_DG))]
    args = [x2, mods, w_in, cos, sin, cs]
    if grid_rows_out:
        n_r = tm // GRID_W
        for o in (7, 8):
            out_specs[o] = pl.BlockSpec((n_r, GRID_W * ch), lambda i: (i, 0))
            out_shape[o] = jax.ShapeDtypeStruct((t // GRID_W, GRID_W * ch), F32)
        src = (np.arange(n_r)[None, :] * GRID_W + np.arange(GRID_W)[:, None]).reshape(-1)
        perm = np.zeros((tm, tm), np.float32)
        perm[np.arange(tm), src] = 1.0
        in_specs.append(_resident((tm, tm)))
        args.append(jnp.asarray(perm).astype(BF16))
    return pl.pallas_call(
        kern,
        grid=(t // tm,),
        in_specs=in_specs,
        out_specs=out_specs,
        out_shape=out_shape,
        compiler_params=_cparams(("parallel",), 56),
        name="mixer_in_proj",
    )(*args)


def _log_sigmoid(x):
    return jnp.minimum(x, 0.0) - jnp.log(1.0 + jnp.exp(-jnp.abs(x)))


def _kv_update(k, v, lf, lb, idx):
    n = k.shape[0]
    kf = k.astype(F32)
    kk = jnp.concatenate([(kf * jnp.exp(lf * (n - 1.0 - idx))).astype(BF16),
                          (kf * jnp.exp(lb * idx)).astype(BF16)], axis=1)
    return _dot_tn(kk, v)


def _ret_kernel(*refs, n_chunks, chunk, with_prefix):
    if with_prefix:
        lg_ref, q_ref, k_ref, v_ref, g_ref, gn_ref, kc_ref, vc_ref, o_ref, u_ref, s_ref = refs
    else:
        lg_ref, q_ref, k_ref, v_ref, g_ref, gn_ref, o_ref, u_ref, s_ref = refs
    c = chunk
    dk, dv = RET_DK, RET_DV
    ls = _log_sigmoid(lg_ref[...])
    lf = ls[0, 0:1, :]
    lb = ls[1, 0:1, :]
    lf_v = jnp.concatenate([lf] * (dv // V7X_LANES), axis=1)
    lb_v = jnp.concatenate([lb] * (dv // V7X_LANES), axis=1)
    lf_c = jnp.concatenate([lf] * (c // V7X_LANES), axis=1)
    lb_c = jnp.concatenate([lb] * (c // V7X_LANES), axis=1)
    idx = lax.broadcasted_iota(jnp.int32, (c, V7X_LANES), 0).astype(F32)

    def upd(i, carry):
        r0 = pl.multiple_of(i * c, c)
        u_ref[i] = _kv_update(k_ref[pl.ds(r0, c), :], v_ref[pl.ds(r0, c), :], lf, lb, idx)
        return carry
    lax.fori_loop(0, n_chunks, upd, 0, unroll=min(RET_UNROLL, n_chunks))

    if with_prefix:
        n_ctx = kc_ref.shape[0]
        idx_c = lax.broadcasted_iota(jnp.int32, (n_ctx, V7X_LANES), 0).astype(F32)
        u0 = _kv_update(kc_ref[...], vc_ref[...], lf, lb, idx_c)
        sf0, sb0 = u0[:dk], u0[dk:]
    else:
        sf0 = jnp.zeros((dk, dv), F32)
        sb0 = sf0
    gf_c = jnp.exp(lf_v * float(c))
    gb_c = jnp.exp(lb_v * float(c))

    def scan_f(i, sf):
        s_ref[i, :dk, :] = sf.astype(BF16)
        return sf * gf_c + u_ref[i, :dk, :]
    lax.fori_loop(0, n_chunks, scan_f, sf0)

    def scan_b(t, sb):
        i = n_chunks - 1 - t
        s_ref[i, dk:, :] = sb.astype(BF16)
        return sb * gb_c + u_ref[i, dk:, :]
    lax.fori_loop(0, n_chunks, scan_b, sb0)

    pr = (lax.broadcasted_iota(jnp.int32, (c, c), 0)
          - lax.broadcasted_iota(jnp.int32, (c, c), 1)).astype(F32)
    dmat = jnp.where(pr >= 0.0, jnp.exp(lf_c * jnp.maximum(pr, 0.0)),
                     jnp.exp(lb_c * jnp.maximum(-pr, 0.0)))
    qdf = jnp.exp(lf * (idx + 1.0))
    qdb = jnp.exp(lb * (float(c) - idx))
    gn = gn_ref[...]
    gn_w, gn_b = gn[0:1], gn[1:2]

    def out(i, carry):
        r0 = pl.multiple_of(i * c, c)
        q = q_ref[pl.ds(r0, c), :]
        k = k_ref[pl.ds(r0, c), :]
        v = v_ref[pl.ds(r0, c), :]
        inner = (_dot_nt(q, k) * dmat).astype(BF16)
        qf = q.astype(F32)
        qq = jnp.concatenate([(qf * qdf).astype(BF16), (qf * qdb).astype(BF16)], axis=1)
        o = _dot(inner, v) + _dot(qq, s_ref[i])
        o = _ln_rows(o) * gn_w + gn_b
        o_ref[pl.ds(r0, c), :] = (o * g_ref[pl.ds(r0, c), :].astype(F32)).astype(BF16)
        return carry
    lax.fori_loop(0, n_chunks, out, 0, unroll=min(RET_UNROLL, n_chunks))


def _retention(lg, q, k, v, g, gn, kc=None, vc=None, *, layer):
    b, n, _ = q.shape
    c = min(RET_CHUNK, n)
    nc = n // c
    with_prefix = kc is not None
    kern = functools.partial(_ret_kernel, n_chunks=nc, chunk=c, with_prefix=with_prefix)
    in_specs = [pl.BlockSpec((None, None, 2, 8, V7X_LANES), lambda bi, hi: (layer, hi, 0, 0, 0)),
                pl.BlockSpec((None, n, RET_DK), lambda bi, hi: (bi, 0, hi)),
                pl.BlockSpec((None, n, RET_DK), lambda bi, hi: (bi, 0, hi)),
                pl.BlockSpec((None, n, RET_DV), lambda bi, hi: (bi, 0, hi)),
                pl.BlockSpec((None, n, RET_DV), lambda bi, hi: (bi, 0, hi)),
                pl.BlockSpec((None, None, 2, RET_DV), lambda bi, hi: (layer, hi, 0, 0))]
    args = [lg, q, k, v, g, gn]
    if with_prefix:
        n_ctx = kc.shape[1]
        in_specs += [pl.BlockSpec((None, n_ctx, RET_DK), lambda bi, hi: (bi, 0, hi)),
                     pl.BlockSpec((None, n_ctx, RET_DV), lambda bi, hi: (bi, 0, hi))]
        args += [kc, vc]
    return pl.pallas_call(
        kern,
        grid=(b, RET_HEADS),
        in_specs=in_specs,
        out_specs=pl.BlockSpec((None, n, RET_DV), lambda bi, hi: (bi, 0, hi)),
        out_shape=jax.ShapeDtypeStruct((b, n, RET_HEADS * RET_DV), BF16),
        scratch_shapes=[pltpu.VMEM((nc, 2 * RET_DK, RET_DV), F32),
                        pltpu.VMEM((nc, 2 * RET_DK, RET_DV), BF16)],
        compiler_params=_cparams(("parallel", "parallel"), 56),
        name="retention",
    )(*args)


def _softmax_pv(s_parts, v_parts):
    m = None
    for s in s_parts:
        mi = jnp.max(s, axis=-1, keepdims=True)
        m = mi if m is None else jnp.maximum(m, mi)
    l = None
    o = None
    for s, v in zip(s_parts, v_parts):
        p = jnp.exp2(s - m)
        li = jnp.sum(p, axis=-1, keepdims=True)
        oi = _dot(p.astype(BF16), v)
        l = li if l is None else l + li
        o = oi if o is None else o + oi
    return o / l


def _na_kernel(q_ref, k_ref, v_ref, kc_ref, vc_ref, bias_ref, o_ref, s_ref, *, n_groups, key_row0_max):
    gq = NA_GROUP_ROWS * GRID_W
    nk = NA_KEY_ROWS * GRID_W
    lane = lax.broadcasted_iota(jnp.int32, (gq, V7X_LANES), 1)
    head0 = lane < NA_DH
    kc = kc_ref[...]
    vc = vc_ref[...]

    def key_start(g):
        kb = jnp.clip(g * NA_GROUP_ROWS - NA_KH // 2, 0, key_row0_max)
        return pl.multiple_of(kb * GRID_W, GRID_W)

    def scores(g, slot):
        q0 = pl.multiple_of(g * gq, gq)
        cfg = jnp.where(g == 0, 0, jnp.where(g == n_groups - 1, 2, 1))
        q = q_ref[pl.ds(q0, gq), :]
        kl = k_ref[pl.ds(key_start(g), nk), :]
        zero = jnp.zeros_like(q)
        for hh in range(2):
            qh = jnp.where(head0, q, zero) if hh == 0 else jnp.where(head0, zero, q)
            s_ref[slot, hh, :, :nk] = _dot_nt(qh, kl) + bias_ref[cfg, hh]
            s_ref[slot, hh, :, nk:] = _dot_nt(qh, kc)

    def finish(g, slot):
        q0 = pl.multiple_of(g * gq, gq)
        vl = v_ref[pl.ds(key_start(g), nk), :]
        outs = [_softmax_pv([s_ref[slot, hh, :, :nk], s_ref[slot, hh, :, nk:]], [vl, vc])
                for hh in range(2)]
        o_ref[pl.ds(q0, gq), :] = jnp.where(head0, outs[0], outs[1]).astype(BF16)

    scores(0, 0)

    def pair(j, carry):
        g = 2 * j
        scores(g + 1, 1)
        finish(g, 0)
        scores(jnp.minimum(g + 2, n_groups - 1), 0)
        finish(g + 1, 1)
        return carry
    lax.fori_loop(0, n_groups // 2, pair, 0, unroll=2)


def _na_bias_tables(rpb):
    h = rpb.shape[0]
    w = GRID_W
    cols = np.arange(w)
    cs = np.clip(cols - NA_KW // 2, 0, w - NA_KW)
    kcol = np.arange(w)[None, :]
    colmask = (kcol >= cs[:, None]) & (kcol < cs[:, None] + NA_KW)
    pad = w - 1
    rpb_p = jnp.pad(rpb, ((0, 0), (0, 0), (pad, pad)))
    toep = jnp.stack([rpb_p[:, :, pad + NA_KW - 1 - c: pad + NA_KW - 1 - c + w] for c in range(w)],
                     axis=2)
    toep = jnp.where(colmask[None, None], toep * LOG2_E, NEG_BIG)
    n_dr = toep.shape[1]
    toep = jnp.concatenate([toep, jnp.full((h, 1, w, w), NEG_BIG, F32)], axis=1)
    dr = np.full((3, NA_GROUP_ROWS, NA_KEY_ROWS), n_dr, np.int32)
    for cfg in range(3):
        rel = NA_GROUP_ROWS * cfg
        for rr in range(NA_GROUP_ROWS):
            lo = (0, rr, NA_KEY_ROWS - NA_KH)[cfg]
            for i in range(lo, lo + NA_KH):
                dr[cfg, rr, i] = i - rel - rr + NA_KH - 1
    toep_c = jnp.transpose(toep, (0, 2, 1, 3))
    blocks = jnp.take(toep_c, jnp.asarray(dr.reshape(-1)), axis=2)
    blocks = blocks.reshape(h, w, 3, NA_GROUP_ROWS, NA_KEY_ROWS * w)
    return jnp.transpose(blocks, (2, 0, 3, 1, 4)).reshape(
        3, h, NA_GROUP_ROWS * w, NA_KEY_ROWS * w)


def _neighbourhood_attention(nq, nk, nv, kc, vc, bias):
    b, n, hw = nq.shape
    rows = n // GRID_W
    n_ctx = kc.shape[1]
    n_groups = rows // NA_GROUP_ROWS
    pairs = NA_HEADS // 2
    gq = NA_GROUP_ROWS * GRID_W
    nkeys = NA_KEY_ROWS * GRID_W
    kern = functools.partial(_na_kernel, n_groups=n_groups, key_row0_max=rows - NA_KEY_ROWS)
    tok = pl.BlockSpec((None, n, V7X_LANES), lambda bi, pi: (bi, 0, pi))
    ctx = pl.BlockSpec((None, n_ctx, V7X_LANES), lambda bi, pi: (bi, 0, pi))
    return pl.pallas_call(
        kern,
        grid=(b, pairs),
        in_specs=[tok, tok, tok, ctx, ctx,
                  pl.BlockSpec((3, 2, gq, nkeys), lambda bi, pi: (0, pi, 0, 0))],
        out_specs=tok,
        out_shape=jax.ShapeDtypeStruct((b, n, hw), BF16),
        scratch_shapes=[pltpu.VMEM((2, 2, gq, nkeys + n_ctx), F32)],
        compiler_params=_cparams(("parallel", "parallel"), 48),
        name="neighbourhood_attention",
    )(nq, nk, nv, kc, vc, bias)


def _ctx_attn_kernel(q_ref, k_ref, v_ref, o_ref):
    q = q_ref[...]
    k = k_ref[...]
    v = v_ref[...]
    lane = lax.broadcasted_iota(jnp.int32, q.shape, 1)
    head0 = lane < NA_DH
    zero = jnp.zeros_like(q)
    outs = []
    for hh in range(2):
        qh = jnp.where(head0, q, zero) if hh == 0 else jnp.where(head0, zero, q)
        outs.append(_softmax_pv([_dot_nt(qh, k)], [v]))
    o_ref[...] = jnp.where(head0, outs[0], outs[1]).astype(BF16)


def _context_attention(q, k, v):
    b, n_ctx, hw = q.shape
    blk = pl.BlockSpec((None, n_ctx, V7X_LANES), lambda bi, pi: (bi, 0, pi))
    return pl.pallas_call(
        _ctx_attn_kernel,
        grid=(b, NA_HEADS // 2),
        in_specs=[blk, blk, blk],
        out_specs=blk,
        out_shape=jax.ShapeDtypeStruct((b, n_ctx, hw), BF16),
        compiler_params=_cparams(("parallel", "parallel"), 32),
        name="context_attention",
    )(q, k, v)


def _dft_rows_kernel(zre_ref, zim_ref, w_ref, are_ref, aim_ref):
    r = zre_ref.shape[0]
    z = jnp.concatenate([zre_ref[...].astype(BF16), zim_ref[...].astype(BF16)], axis=0)
    a = _dot(w_ref[...], z)
    are_ref[...] = a[:r]
    aim_ref[...] = a[r:]


def _dft_rows(zre, zim, w2):
    b, r, wc = zre.shape
    cb = min(4096, wc)
    blk = pl.BlockSpec((None, r, cb), lambda bi, ci: (bi, 0, ci))
    return pl.pallas_call(
        _dft_rows_kernel,
        grid=(b, wc // cb),
        in_specs=[blk, blk, _resident((2 * r, 2 * r))],
        out_specs=[blk, blk],
        out_shape=[jax.ShapeDtypeStruct((b, r, wc), F32)] * 2,
        compiler_params=_cparams(("parallel", "parallel"), 40),
        name="dft_rows",
    )(zre, zim, w2)


def _dft_cols_kernel(are_ref, aim_ref, twc_ref, tws_ref, km_ref, wo_ref, o_ref, *, scale, ch):
    w = are_ref.shape[1] // ch
    ar = jnp.concatenate([are_ref[:, c * ch:(c + 1) * ch] for c in range(w)], axis=0)
    ai = jnp.concatenate([aim_ref[:, c * ch:(c + 1) * ch] for c in range(w)], axis=0)
    reps = ch // V7X_LANES
    tc = jnp.concatenate([twc_ref[...]] * reps, axis=1)
    ts = jnp.concatenate([tws_ref[...]] * reps, axis=1)
    a = jnp.concatenate([(ar * tc - ai * ts).astype(BF16), (ar * ts + ai * tc).astype(BF16)], axis=0)
    y = _dot(km_ref[...], a) * scale
    out = _dot(y.astype(BF16), wo_ref[...]).astype(BF16)
    d = out.shape[1]
    for j in range(out.shape[0] // w):
        o_ref[:, j * d:(j + 1) * d] = out[j * w:(j + 1) * w, :]


def _dft_cols(a_re, a_im, twc, tws, kmat, w_out, *, layer, scale):
    b, r, wc = a_re.shape
    w = GRID_W
    ch = wc // w
    d = w_out.shape[2]
    jb = FOU_K1_BLOCK
    blk = pl.BlockSpec((None, jb, wc), lambda bi, ji: (bi, ji, 0))
    tw = pl.BlockSpec((jb * w, V7X_LANES), lambda bi, ji: (ji, 0))
    kern = functools.partial(_dft_cols_kernel, scale=scale, ch=ch)
    return pl.pallas_call(
        kern,
        grid=(b, r // jb),
        in_specs=[blk, blk, tw, tw, _resident(kmat.shape), _resident((ch, d), layer)],
        out_specs=pl.BlockSpec((None, w, jb * d), lambda bi, ji: (bi, 0, ji)),
        out_shape=jax.ShapeDtypeStruct((b, w, r * d), BF16),
        compiler_params=_cparams(("parallel", "parallel"), 32),
        name="dft_cols",
    )(a_re, a_im, twc, tws, kmat, w_out)


def _ctx_fourier_kernel(zre_ref, zim_ref, cm_ref, wo_ref, o_ref, *, scale):
    z = jnp.concatenate([zre_ref[...], zim_ref[...]], axis=0)
    y = _dot(cm_ref[...], z) * scale
    o_ref[...] = _dot(y.astype(BF16), wo_ref[...]).astype(BF16)


def _ctx_fourier(zre, zim, cmat, w_out, *, layer, scale):
    b, n_ctx, ch = zre.shape
    d = w_out.shape[2]
    blk = pl.BlockSpec((None, n_ctx, ch), lambda bi: (bi, 0, 0))
    kern = functools.partial(_ctx_fourier_kernel, scale=scale)
    return pl.pallas_call(
        kern,
        grid=(b,),
        in_specs=[blk, blk, _resident(cmat.shape), _resident((ch, d), layer)],
        out_specs=pl.BlockSpec((None, n_ctx, d), lambda bi: (bi, 0, 0)),
        out_shape=jax.ShapeDtypeStruct((b, n_ctx, d), BF16),
        compiler_params=_cparams(("parallel",), 32),
        name="ctx_fourier",
    )(zre, zim, cmat, w_out)


def _merge_kernel(x_ref, mods_ref, ret_ref, na_ref, fou_ref, gt_ref, wr_ref, wn_ref, wo_ref, ln_ref,
                  o_ref, *, alpha):
    d = x_ref.shape[1]
    m = mods_ref[...]
    ln = ln_ref[...]
    for rows in _row_splits(x_ref.shape[0]):
        y_ret = _dot(ret_ref[rows, :], wr_ref[...])
        y_na = _dot(na_ref[rows, :], wn_ref[...])
        mix = (gt_ref[rows, 0:d].astype(F32) * y_ret + gt_ref[rows, d:2 * d].astype(F32) * y_na
               + gt_ref[rows, 2 * d:3 * d].astype(F32) * fou_ref[rows, :].astype(F32))
        y = _dot(mix.astype(BF16), wo_ref[...])
        z = alpha * x_ref[rows, :] + m[5:6] * y
        o_ref[rows, :] = _ln_rows(z) * ln[1:2] + ln[4:5]


def _merge(x2, mods, mods_row, ret, na, fou, gt, wr, wn, wo, ln, *, layer, tm, alpha):
    t, d = x2.shape
    kern = functools.partial(_merge_kernel, alpha=alpha)

    def rows(wd):
        return pl.BlockSpec((tm, wd), lambda i: (i, 0))
    return pl.pallas_call(
        kern,
        grid=(t // tm,),
        in_specs=[rows(d), pl.BlockSpec((None, None, 9, d), lambda i: (layer, mods_row(i), 0, 0)),
                  rows(ret.shape[1]), rows(na.shape[1]), rows(d), rows(gt.shape[1]),
                  _resident(wr.shape[1:], layer), _resident(wn.shape[1:], layer),
                  _resident(wo.shape[1:], layer), _resident((6, d), layer)],
        out_specs=rows(d),
        out_shape=jax.ShapeDtypeStruct((t, d), F32),
        compiler_params=_cparams(("parallel",), 60 if tm > TOKEN_TILE else 48),
        name="merge",
    )(x2, mods, ret, na, fou, gt, wr, wn, wo, ln)


def _rope_tables(n):
    half = RET_DK // 4
    t = np.arange(n)
    row, col = t // GRID_W, t % GRID_W
    freqs = ROPE_BASE ** (-np.arange(half, dtype=np.float64) / half)
    ang_r = row.astype(np.float64)[:, None] * freqs[None, :]
    ang_c = col.astype(np.float64)[:, None] * freqs[None, :]
    cos = np.concatenate([np.cos(ang_r)] * 2 + [np.cos(ang_c)] * 2, axis=1)
    sin = np.concatenate([-np.sin(ang_r), np.sin(ang_r), -np.sin(ang_c), np.sin(ang_c)], axis=1)
    return jnp.asarray(cos, F32), jnp.asarray(sin, F32)


def _dft_tables(n):
    w = GRID_W
    r = n // w
    jb = FOU_K1_BLOCK
    two_pi = 2.0 * np.pi
    ch = np.arange(FOU_DG)
    ang = two_pi * ((ch[:, None] * ch[None, :]) % FOU_DG) / FOU_DG
    cs = np.concatenate([np.cos(ang), np.sin(ang)], axis=1)
    k1 = np.arange(r)
    ang = two_pi * ((k1[:, None] * k1[None, :]) % r) / r
    w2 = np.block([[np.cos(ang), -np.sin(ang)], [np.sin(ang), np.cos(ang)]])
    c = np.arange(w)
    k1_blocks = k1.reshape(r // jb, 1, jb)
    ang = two_pi * (k1_blocks * c[None, :, None]).reshape(-1) / n
    twc = np.repeat(np.cos(ang)[:, None], V7X_LANES, axis=1)
    tws = np.repeat(np.sin(ang)[:, None], V7X_LANES, axis=1)
    ang = two_pi * ((c[:, None] * c[None, :]) % w) / w
    eye = np.eye(jb)
    kc = np.einsum("kc,ji->jkci", np.cos(ang), eye).reshape(w * jb, w * jb)
    ks = np.einsum("kc,ji->jkci", np.sin(ang), eye).reshape(w * jb, w * jb)
    kmat = np.concatenate([kc, -ks], axis=1)
    as_bf16 = lambda a: jnp.asarray(a, F32).astype(BF16)
    return (as_bf16(cs), as_bf16(w2), jnp.asarray(twc, F32), jnp.asarray(tws, F32), as_bf16(kmat))


def _ctx_dft_matrix(n_ctx):
    m = np.arange(n_ctx)
    ang = 2.0 * np.pi * ((m[:, None] * m[None, :]) % n_ctx) / n_ctx
    return jnp.asarray(np.concatenate([np.cos(ang), -np.sin(ang)], axis=1), F32).astype(BF16)


def kernel(x, c, ctx, c_ctx, ada_w, ada_b, ln_w, ln_b, ffa_w1, ffa_w3, ffa_w2, mix_w_in,
           ret_decay_logit, ret_gn_w, ret_gn_b, ret_w_out, na_rpb, na_w_out, fou_w_out, mix_w_o,
           ffb_w1, ffb_w3, ffb_w2):
    b, n, d = x.shape
    n_ctx = ctx.shape[1]
    depth = ada_w.shape[0]
    rows = n // GRID_W
    assert n % TOKEN_TILE == 0 and rows % (2 * NA_GROUP_ROWS) == 0 and rows >= NA_KEY_ROWS
    assert rows % FOU_K1_BLOCK == 0 and b + 1 <= 8
    alpha = (2 * depth) ** 0.25
    tm = TOKEN_TILE
    tm_ffn = FFN_TILE if n % FFN_TILE == 0 else TOKEN_TILE
    tiles_per_seq = n // tm
    lat_row = lambda i: i // tiles_per_seq
    ffn_row = lambda i: i // (n // tm_ffn)
    ctx_row = lambda i: b
    lat_rope = lambda i: i % tiles_per_seq
    ctx_rope = lambda i: 0

    cos_x, sin_x = _rope_tables(n)
    cos_c = jnp.ones((n_ctx, V7X_LANES), F32)
    sin_c = jnp.zeros((n_ctx, V7X_LANES), F32)
    cs, w2, twc, tws, kmat = _dft_tables(n)
    cmat_ctx = _ctx_dft_matrix(n_ctx)
    fou_scale_x = 1.0 / math.sqrt(n * FOU_DG)
    fou_scale_c = 1.0 / math.sqrt(n_ctx * FOU_DG)

    cvec = jnp.concatenate([c, c_ctx[None, :], jnp.zeros((8 - b - 1, d), F32)], axis=0)
    mods_all = _ada_mods(cvec, ada_w, ada_b).reshape(depth, 8, 9, d)
    ln_all = jnp.concatenate([ln_w, ln_b], axis=1)
    to_bf16 = lambda a: a.astype(BF16)
    ffa = [to_bf16(a) for a in (ffa_w1, ffa_w3, ffa_w2)]
    ffb = [to_bf16(a) for a in (ffb_w1, ffb_w3, ffb_w2)]
    w_in, wr, wn, wf, wo = [to_bf16(a) for a in (mix_w_in, ret_w_out, na_w_out, fou_w_out, mix_w_o)]
    lg_all = jnp.broadcast_to(
        jnp.transpose(ret_decay_logit, (0, 2, 1))[:, :, :, None, None],
        (depth, RET_HEADS, 2, 8, V7X_LANES)).astype(F32)
    gn_all = jnp.stack([ret_gn_w.reshape(depth, RET_HEADS, RET_DV),
                        ret_gn_b.reshape(depth, RET_HEADS, RET_DV)], axis=2)

    xs = x.reshape(b * n, d)
    cx = ctx.reshape(b * n_ctx, d)
    mods, ln = mods_all, ln_all
    for l in range(depth):
        last = l == depth - 1
        xs = _ffn(xs, mods, ffn_row, *ffa, ln, layer=l, tm=tm_ffn, mod_base=0, ln_row=0, alpha=alpha)
        cx = _ffn(cx, mods, ctx_row, *ffa, ln, layer=l, tm=n_ctx, mod_base=0, ln_row=0, alpha=alpha)

        px = _proj(xs, mods, lat_row, w_in, cos_x, sin_x, lat_rope, cs, layer=l, tm=tm,
                   grid_rows_out=True)
        pc = _proj(cx, mods, ctx_row, w_in, cos_c, sin_c, ctx_rope, cs, layer=l, tm=n_ctx,
                   grid_rows_out=False)
        q_x, k_x, v_x, g_x, nq_x, nk_x, nv_x = [a.reshape(b, n, -1) for a in px[:7]]
        gt_x = px[9]
        q_c, k_c, v_c, g_c, nq_c, nk_c, nv_c, zre_c, zim_c, gt_c = [a.reshape(b, n_ctx, -1) for a in pc]

        ret_x = _retention(lg_all, q_x, k_x, v_x, g_x, gn_all, k_c, v_c, layer=l)
        bias = _na_bias_tables(na_rpb[l])
        na_x = _neighbourhood_attention(nq_x, nk_x, nv_x, nk_c, nv_c, bias)
        a_re, a_im = _dft_rows(px[7].reshape(b, rows, -1), px[8].reshape(b, rows, -1), w2)
        fou_x = _dft_cols(a_re, a_im, twc, tws, kmat, wf, layer=l, scale=fou_scale_x)
        xs = _merge(xs, mods, ffn_row, ret_x.reshape(b * n, -1), na_x.reshape(b * n, -1),
                    fou_x.reshape(b * n, d), gt_x, wr, wn, wo, ln,
                    layer=l, tm=tm_ffn, alpha=alpha)
        xs = _ffn(xs, mods, ffn_row, *ffb, ln, layer=l, tm=tm_ffn, mod_base=6, ln_row=2, alpha=alpha)
        if not last:
            ret_c = _retention(lg_all, q_c, k_c, v_c, g_c, gn_all, layer=l)
            na_c = _context_attention(nq_c, nk_c, nv_c)
            fou_c = _ctx_fourier(zre_c, zim_c, cmat_ctx, wf, layer=l, scale=fou_scale_c)
            cx = _merge(cx, mods, ctx_row, ret_c.reshape(b * n_ctx, -1), na_c.reshape(b * n_ctx, -1),
                        fou_c.reshape(b * n_ctx, d), gt_c.reshape(b * n_ctx, -1), wr, wn, wo, ln,
                        layer=l, tm=n_ctx, alpha=alpha)
            cx = _ffn(cx, mods, ctx_row, *ffb, ln, layer=l, tm=n_ctx, mod_base=6, ln_row=2,
                      alpha=alpha)
    return xs.reshape(b, n, d)
```

```python
import functools
import math

import numpy as np
import jax
import jax.numpy as jnp
from jax import lax
from jax.experimental import pallas as pl
from jax.experimental.pallas import tpu as pltpu

F32 = jnp.float32
BF16 = jnp.bfloat16

GRID_W = 64
RET_HEADS = 4
RET_DK = 128
RET_DV = 256
NA_HEADS = 8
NA_DH = 64
NA_KH = 8
NA_KW = 16
FOU_GROUPS = 4
FOU_DG = 128
N_BRANCH = 3
ROPE_BASE = 10000.0
LN_EPS = 1e-6

V7X_VMEM_BYTES = 64 * 1024 * 1024
V7X_LANES = 128
V7X_SUBLANES = 8

TOKEN_TILE = 512
RET_CHUNK = 256
FFN_TILE = 1024
TOKEN_SPLIT = 4
SUBTILE_MIN = 256
RET_UNROLL = 8
NA_GROUP_ROWS = 4
NA_KEY_ROWS = NA_GROUP_ROWS + NA_KH
FOU_K1_BLOCK = V7X_SUBLANES
NEG_BIG = -1e30
LOG2_E = math.log2(math.e)


def _cparams(sem, vmem_mb):
    return pltpu.CompilerParams(dimension_semantics=sem,
                                vmem_limit_bytes=int(vmem_mb * 1024 * 1024))


def _resident(shape, layer=None):
    nd = len(shape)
    if layer is None:
        return pl.BlockSpec(shape, lambda *_: (0,) * nd, pipeline_mode=pl.Buffered(1))
    return pl.BlockSpec((None,) + tuple(shape), lambda *_: (layer,) + (0,) * nd,
                        pipeline_mode=pl.Buffered(1))


def _row_splits(tm):
    n = max(1, min(TOKEN_SPLIT, tm // SUBTILE_MIN))
    step = tm // n
    return [slice(s * step, (s + 1) * step) for s in range(n)]


def _ln_rows(x):
    mu = jnp.mean(x, axis=-1, keepdims=True)
    xc = x - mu
    var = jnp.mean(xc * xc, axis=-1, keepdims=True)
    return xc * lax.rsqrt(var + LN_EPS)


def _sigmoid(x):
    return 1.0 / (1.0 + jnp.exp(-x))


def _dot(a, b):
    return jnp.dot(a, b, preferred_element_type=F32)


def _dot_nt(a, b):
    return lax.dot_general(a, b, (((1,), (1,)), ((), ())), preferred_element_type=F32)


def _dot_tn(a, b):
    return lax.dot_general(a, b, (((0,), (0,)), ((), ())), preferred_element_type=F32)


def _mods_kernel(c_ref, w_ref, b_ref, o_ref):
    c = c_ref[...]
    h = (c * _sigmoid(c)).astype(BF16)
    o_ref[0] = _dot(h, w_ref[0].astype(BF16)) + b_ref[0]


def _ada_mods(cvec, ada_w, ada_b):
    depth, d, nine_d = ada_w.shape
    wb = 3 * d
    nblk = nine_d // wb
    return pl.pallas_call(
        _mods_kernel,
        grid=(depth, nblk),
        in_specs=[pl.BlockSpec((8, d), lambda l, j: (0, 0)),
                  pl.BlockSpec((1, d, wb), lambda l, j: (l, 0, j)),
                  pl.BlockSpec((1, 1, wb), lambda l, j: (l, 0, j))],
        out_specs=pl.BlockSpec((1, 8, wb), lambda l, j: (l, 0, j)),
        out_shape=jax.ShapeDtypeStruct((depth, 8, nine_d), F32),
        compiler_params=_cparams(("parallel", "parallel"), 40),
        name="ada_mods",
    )(cvec, ada_w, ada_b.reshape(depth, 1, nine_d))


def _ffn_kernel(x_ref, mods_ref, w1_ref, w3_ref, w2_ref, ln_ref, o_ref, *, mod_base, ln_row, alpha):
    m = mods_ref[...]
    shift = m[mod_base:mod_base + 1]
    scale = m[mod_base + 1:mod_base + 2]
    gate = m[mod_base + 2:mod_base + 3]
    ln = ln_ref[...]
    for rows in _row_splits(x_ref.shape[0]):
        x = x_ref[rows, :]
        h = (_ln_rows(x) * (1.0 + scale) + shift).astype(BF16)
        a = _dot(h, w1_ref[...])
        b = _dot(h, w3_ref[...])
        g = (a * _sigmoid(a) * b).astype(BF16)
        y = _dot(g, w2_ref[...])
        z = alpha * x + 0.5 * gate * y
        o_ref[rows, :] = _ln_rows(z) * ln[ln_row:ln_row + 1] + ln[3 + ln_row:4 + ln_row]


def _ffn(x2, mods, mods_row, w1, w3, w2, ln, *, layer, tm, mod_base, ln_row, alpha):
    t, d = x2.shape
    dff = w1.shape[2]
    kern = functools.partial(_ffn_kernel, mod_base=mod_base, ln_row=ln_row, alpha=alpha)
    return pl.pallas_call(
        kern,
        grid=(t // tm,),
        in_specs=[pl.BlockSpec((tm, d), lambda i: (i, 0)),
                  pl.BlockSpec((None, None, 9, d), lambda i: (layer, mods_row(i), 0, 0)),
                  _resident((d, dff), layer), _resident((d, dff), layer), _resident((dff, d), layer),
                  _resident((6, d), layer)],
        out_specs=pl.BlockSpec((tm, d), lambda i: (i, 0)),
        out_shape=jax.ShapeDtypeStruct((t, d), F32),
        compiler_params=_cparams(("parallel",), 62 if tm > TOKEN_TILE else 56),
        name="ffn",
    )(x2, mods, w1, w3, w2, ln)


def _rope(t, cos, sin_signed, first_half):
    swapped = jnp.where(first_half, pltpu.roll(t, 96, 1), pltpu.roll(t, 32, 1))
    return t * cos + swapped * sin_signed


def _proj_kernel(x_ref, mods_ref, w_ref, cos_ref, sin_ref, cs_ref, *rest,
                 q_scale, nq_scale, grid_rows_out):
    extra = rest[:-10]
    q_ref, k_ref, v_ref, g_ref, nq_ref, nk_ref, nv_ref, zre_ref, zim_ref, gt_ref = rest[-10:]
    x = x_ref[...]
    m = mods_ref[...]
    h = (_ln_rows(x) * (1.0 + m[4:5]) + m[3:4]).astype(BF16)
    tm = x.shape[0]
    cw = 512

    def seg(j):
        return _dot(h, w_ref[:, j * cw:(j + 1) * cw])

    cos = cos_ref[...]
    sin = sin_ref[...]
    lane = lax.broadcasted_iota(jnp.int32, (tm, V7X_LANES), 1)
    first_half = (lane & 32) == 0

    cs = cs_ref[...]
    u = seg(9).astype(BF16)
    r = seg(0) * q_scale
    for hh in range(RET_HEADS):
        sl = slice(hh * RET_DK, (hh + 1) * RET_DK)
        q_ref[:, sl] = _rope(r[:, sl], cos, sin, first_half).astype(BF16)
    if grid_rows_out:
        perm_ref, = extra
        u = _dot(perm_ref[...], u).astype(BF16)
    r = seg(1)
    for hh in range(RET_HEADS):
        sl = slice(hh * RET_DK, (hh + 1) * RET_DK)
        k_ref[:, sl] = _rope(r[:, sl], cos, sin, first_half).astype(BF16)
    n_r = tm // GRID_W

    def fourier(gi):
        sl = slice(gi * FOU_DG, (gi + 1) * FOU_DG)
        z = _dot(u[:, sl], cs)
        if grid_rows_out:
            for c in range(GRID_W):
                dst = slice((c * FOU_GROUPS + gi) * FOU_DG, (c * FOU_GROUPS + gi + 1) * FOU_DG)
                zre_ref[:, dst] = z[c * n_r:(c + 1) * n_r, :FOU_DG]
                zim_ref[:, dst] = z[c * n_r:(c + 1) * n_r, FOU_DG:]
        else:
            zre_ref[:, sl] = z[:, :FOU_DG].astype(BF16)
            zim_ref[:, sl] = z[:, FOU_DG:].astype(BF16)

    for j in range(2):
        r = seg(4 + j)
        g_ref[:, j * cw:(j + 1) * cw] = (r * _sigmoid(r)).astype(BF16)
        fourier(j)
    for j in range(6):
        gt_ref[:, j * cw:(j + 1) * cw] = _sigmoid(seg(10 + j)).astype(BF16)
        if j < FOU_GROUPS - 2:
            fourier(2 + j)
    nq_ref[...] = (seg(6) * nq_scale).astype(BF16)
    for j in range(2):
        v_ref[:, j * cw:(j + 1) * cw] = seg(2 + j).astype(BF16)
    nk_ref[...] = seg(7).astype(BF16)
    nv_ref[...] = seg(8).astype(BF16)


def _proj(x2, mods, mods_row, w_in, cos, sin, rope_row, cs, *, layer, tm, grid_rows_out):
    t, d = x2.shape
    d_in = w_in.shape[2]
    ch = FOU_GROUPS * FOU_DG
    widths = (512, 512, 1024, 1024, 512, 512, 512, ch, ch, 3072)
    kern = functools.partial(_proj_kernel, q_scale=RET_DK ** -0.5, nq_scale=NA_DH ** -0.5 * LOG2_E,
                             grid_rows_out=grid_rows_out)
    out_specs = [pl.BlockSpec((tm, w), lambda i: (i, 0)) for w in widths]
    out_shape = [jax.ShapeDtypeStruct((t, w), BF16) for w in widths]
    in_specs = [pl.BlockSpec((tm, d), lambda i: (i, 0)),
                pl.BlockSpec((None, None, 9, d), lambda i: (layer, mods_row(i), 0, 0)),
                _resident((d, d_in), layer),
                pl.BlockSpec((tm, V7X_LANES), lambda i: (rope_row(i), 0)),
                pl.BlockSpec((tm, V7X_LANES), lambda i: (rope_row(i), 0)),
                _resident((FOU_DG, 2 * FOU_DG))]
    args = [x2, mods, w_in, cos, sin, cs]
    if grid_rows_out:
        n_r = tm // GRID_W
        for o in (7, 8):
            out_specs[o] = pl.BlockSpec((n_r, GRID_W * ch), lambda i: (i, 0))
            out_shape[o] = jax.ShapeDtypeStruct((t // GRID_W, GRID_W * ch), F32)
        src = (np.arange(n_r)[None, :] * GRID_W + np.arange(GRID_W)[:, None]).reshape(-1)
        perm = np.zeros((tm, tm), np.float32)
        perm[np.arange(tm), src] = 1.0
        in_specs.append(_resident((tm, tm)))
        args.append(jnp.asarray(perm).astype(BF16))
    return pl.pallas_call(
        kern,
        grid=(t // tm,),
        in_specs=in_specs,
        out_specs=out_specs,
        out_shape=out_shape,
        compiler_params=_cparams(("parallel",), 56),
        name="mixer_in_proj",
    )(*args)


def _log_sigmoid(x):
    return jnp.minimum(x, 0.0) - jnp.log(1.0 + jnp.exp(-jnp.abs(x)))


def _kv_update(k, v, lf, lb, idx):
    n = k.shape[0]
    kf = k.astype(F32)
    kk = jnp.concatenate([(kf * jnp.exp(lf * (n - 1.0 - idx))).astype(BF16),
                          (kf * jnp.exp(lb * idx)).astype(BF16)], axis=1)
    return _dot_tn(kk, v)


def _ret_kernel(*refs, n_chunks, chunk, with_prefix):
    if with_prefix:
        lg_ref, q_ref, k_ref, v_ref, g_ref, gn_ref, kc_ref, vc_ref, o_ref, u_ref, s_ref = refs
    else:
        lg_ref, q_ref, k_ref, v_ref, g_ref, gn_ref, o_ref, u_ref, s_ref = refs
    c = chunk
    dk, dv = RET_DK, RET_DV
    ls = _log_sigmoid(lg_ref[...])
    lf = ls[0, 0:1, :]
    lb = ls[1, 0:1, :]
    lf_v = jnp.concatenate([lf] * (dv // V7X_LANES), axis=1)
    lb_v = jnp.concatenate([lb] * (dv // V7X_LANES), axis=1)
    lf_c = jnp.concatenate([lf] * (c // V7X_LANES), axis=1)
    lb_c = jnp.concatenate([lb] * (c // V7X_LANES), axis=1)
    idx = lax.broadcasted_iota(jnp.int32, (c, V7X_LANES), 0).astype(F32)

    def upd(i, carry):
        r0 = pl.multiple_of(i * c, c)
        u_ref[i] = _kv_update(k_ref[pl.ds(r0, c), :], v_ref[pl.ds(r0, c), :], lf, lb, idx)
        return carry
    lax.fori_loop(0, n_chunks, upd, 0, unroll=min(RET_UNROLL, n_chunks))

    if with_prefix:
        n_ctx = kc_ref.shape[0]
        idx_c = lax.broadcasted_iota(jnp.int32, (n_ctx, V7X_LANES), 0).astype(F32)
        u0 = _kv_update(kc_ref[...], vc_ref[...], lf, lb, idx_c)
        sf0, sb0 = u0[:dk], u0[dk:]
    else:
        sf0 = jnp.zeros((dk, dv), F32)
        sb0 = sf0
    gf_c = jnp.exp(lf_v * float(c))
    gb_c = jnp.exp(lb_v * float(c))

    def scan_f(i, sf):
        s_ref[i, :dk, :] = sf.astype(BF16)
        return sf * gf_c + u_ref[i, :dk, :]
    lax.fori_loop(0, n_chunks, scan_f, sf0)

    def scan_b(t, sb):
        i = n_chunks - 1 - t
        s_ref[i, dk:, :] = sb.astype(BF16)
        return sb * gb_c + u_ref[i, dk:, :]
    lax.fori_loop(0, n_chunks, scan_b, sb0)

    pr = (lax.broadcasted_iota(jnp.int32, (c, c), 0)
          - lax.broadcasted_iota(jnp.int32, (c, c), 1)).astype(F32)
    dmat = jnp.where(pr >= 0.0, jnp.exp(lf_c * jnp.maximum(pr, 0.0)),
                     jnp.exp(lb_c * jnp.maximum(-pr, 0.0)))
    qdf = jnp.exp(lf * (idx + 1.0))
    qdb = jnp.exp(lb * (float(c) - idx))
    gn = gn_ref[...]
    gn_w, gn_b = gn[0:1], gn[1:2]

    def out(i, carry):
        r0 = pl.multiple_of(i * c, c)
        q = q_ref[pl.ds(r0, c), :]
        k = k_ref[pl.ds(r0, c), :]
        v = v_ref[pl.ds(r0, c), :]
        inner = (_dot_nt(q, k) * dmat).astype(BF16)
        qf = q.astype(F32)
        qq = jnp.concatenate([(qf * qdf).astype(BF16), (qf * qdb).astype(BF16)], axis=1)
        o = _dot(inner, v) + _dot(qq, s_ref[i])
        o = _ln_rows(o) * gn_w + gn_b
        o_ref[pl.ds(r0, c), :] = (o * g_ref[pl.ds(r0, c), :].astype(F32)).astype(BF16)
        return carry
    lax.fori_loop(0, n_chunks, out, 0, unroll=min(RET_UNROLL, n_chunks))


def _retention(lg, q, k, v, g, gn, kc=None, vc=None, *, layer):
    b, n, _ = q.shape
    c = min(RET_CHUNK, n)
    nc = n // c
    with_prefix = kc is not None
    kern = functools.partial(_ret_kernel, n_chunks=nc, chunk=c, with_prefix=with_prefix)
    in_specs = [pl.BlockSpec((None, None, 2, 8, V7X_LANES), lambda bi, hi: (layer, hi, 0, 0, 0)),
                pl.BlockSpec((None, n, RET_DK), lambda bi, hi: (bi, 0, hi)),
                pl.BlockSpec((None, n, RET_DK), lambda bi, hi: (bi, 0, hi)),
                pl.BlockSpec((None, n, RET_DV), lambda bi, hi: (bi, 0, hi)),
                pl.BlockSpec((None, n, RET_DV), lambda bi, hi: (bi, 0, hi)),
                pl.BlockSpec((None, None, 2, RET_DV), lambda bi, hi: (layer, hi, 0, 0))]
    args = [lg, q, k, v, g, gn]
    if with_prefix:
        n_ctx = kc.shape[1]
        in_specs += [pl.BlockSpec((None, n_ctx, RET_DK), lambda bi, hi: (bi, 0, hi)),
                     pl.BlockSpec((None, n_ctx, RET_DV), lambda bi, hi: (bi, 0, hi))]
        args += [kc, vc]
    return pl.pallas_call(
        kern,
        grid=(b, RET_HEADS),
        in_specs=in_specs,
        out_specs=pl.BlockSpec((None, n, RET_DV), lambda bi, hi: (bi, 0, hi)),
        out_shape=jax.ShapeDtypeStruct((b, n, RET_HEADS * RET_DV), BF16),
        scratch_shapes=[pltpu.VMEM((nc, 2 * RET_DK, RET_DV), F32),
                        pltpu.VMEM((nc, 2 * RET_DK, RET_DV), BF16)],
        compiler_params=_cparams(("parallel", "parallel"), 56),
        name="retention",
    )(*args)


def _softmax_pv(s_parts, v_parts):
    m = None
    for s in s_parts:
        mi = jnp.max(s, axis=-1, keepdims=True)
        m = mi if m is None else jnp.maximum(m, mi)
    l = None
    o = None
    for s, v in zip(s_parts, v_parts):
        p = jnp.exp2(s - m)
        li = jnp.sum(p, axis=-1, keepdims=True)
        oi = _dot(p.astype(BF16), v)
        l = li if l is None else l + li
        o = oi if o is None else o + oi
    return o / l


def _na_kernel(q_ref, k_ref, v_ref, kc_ref, vc_ref, bias_ref, o_ref, s_ref, *, n_groups, key_row0_max):
    gq = NA_GROUP_ROWS * GRID_W
    nk = NA_KEY_ROWS * GRID_W
    lane = lax.broadcasted_iota(jnp.int32, (gq, V7X_LANES), 1)
    head0 = lane < NA_DH
    kc = kc_ref[...]
    vc = vc_ref[...]

    def key_start(g):
        kb = jnp.clip(g * NA_GROUP_ROWS - NA_KH // 2, 0, key_row0_max)
        return pl.multiple_of(kb * GRID_W, GRID_W)

    def scores(g, slot):
        q0 = pl.multiple_of(g * gq, gq)
        cfg = jnp.where(g == 0, 0, jnp.where(g == n_groups - 1, 2, 1))
        q = q_ref[pl.ds(q0, gq), :]
        kl = k_ref[pl.ds(key_start(g), nk), :]
        zero = jnp.zeros_like(q)
        for hh in range(2):
            qh = jnp.where(head0, q, zero) if hh == 0 else jnp.where(head0, zero, q)
            s_ref[slot, hh, :, :nk] = _dot_nt(qh, kl) + bias_ref[cfg, hh]
            s_ref[slot, hh, :, nk:] = _dot_nt(qh, kc)

    def finish(g, slot):
        q0 = pl.multiple_of(g * gq, gq)
        vl = v_ref[pl.ds(key_start(g), nk), :]
        outs = [_softmax_pv([s_ref[slot, hh, :, :nk], s_ref[slot, hh, :, nk:]], [vl, vc])
                for hh in range(2)]
        o_ref[pl.ds(q0, gq), :] = jnp.where(head0, outs[0], outs[1]).astype(BF16)

    scores(0, 0)

    def pair(j, carry):
        g = 2 * j
        scores(g + 1, 1)
        finish(g, 0)
        scores(jnp.minimum(g + 2, n_groups - 1), 0)
        finish(g + 1, 1)
        return carry
    lax.fori_loop(0, n_groups // 2, pair, 0, unroll=2)


def _na_bias_tables(rpb):
    h = rpb.shape[0]
    w = GRID_W
    cols = np.arange(w)
    cs = np.clip(cols - NA_KW // 2, 0, w - NA_KW)
    kcol = np.arange(w)[None, :]
    colmask = (kcol >= cs[:, None]) & (kcol < cs[:, None] + NA_KW)
    pad = w - 1
    rpb_p = jnp.pad(rpb, ((0, 0), (0, 0), (pad, pad)))
    toep = jnp.stack([rpb_p[:, :, pad + NA_KW - 1 - c: pad + NA_KW - 1 - c + w] for c in range(w)],
                     axis=2)
    toep = jnp.where(colmask[None, None], toep * LOG2_E, NEG_BIG)
    n_dr = toep.shape[1]
    toep = jnp.concatenate([toep, jnp.full((h, 1, w, w), NEG_BIG, F32)], axis=1)
    dr = np.full((3, NA_GROUP_ROWS, NA_KEY_ROWS), n_dr, np.int32)
    for cfg in range(3):
        rel = NA_GROUP_ROWS * cfg
        for rr in range(NA_GROUP_ROWS):
            lo = (0, rr, NA_KEY_ROWS - NA_KH)[cfg]
            for i in range(lo, lo + NA_KH):
                dr[cfg, rr, i] = i - rel - rr + NA_KH - 1
    toep_c = jnp.transpose(toep, (0, 2, 1, 3))
    blocks = jnp.take(toep_c, jnp.asarray(dr.reshape(-1)), axis=2)
    blocks = blocks.reshape(h, w, 3, NA_GROUP_ROWS, NA_KEY_ROWS * w)
    return jnp.transpose(blocks, (2, 0, 3, 1, 4)).reshape(
        3, h, NA_GROUP_ROWS * w, NA_KEY_ROWS * w)


def _neighbourhood_attention(nq, nk, nv, kc, vc, bias):
    b, n, hw = nq.shape
    rows = n // GRID_W
    n_ctx = kc.shape[1]
    n_groups = rows // NA_GROUP_ROWS
    pairs = NA_HEADS // 2
    gq = NA_GROUP_ROWS * GRID_W
    nkeys = NA_KEY_ROWS * GRID_W
    kern = functools.partial(_na_kernel, n_groups=n_groups, key_row0_max=rows - NA_KEY_ROWS)
    tok = pl.BlockSpec((None, n, V7X_LANES), lambda bi, pi: (bi, 0, pi))
    ctx = pl.BlockSpec((None, n_ctx, V7X_LANES), lambda bi, pi: (bi, 0, pi))
    return pl.pallas_call(
        kern,
        grid=(b, pairs),
        in_specs=[tok, tok, tok, ctx, ctx,
                  pl.BlockSpec((3, 2, gq, nkeys), lambda bi, pi: (0, pi, 0, 0))],
        out_specs=tok,
        out_shape=jax.ShapeDtypeStruct((b, n, hw), BF16),
        scratch_shapes=[pltpu.VMEM((2, 2, gq, nkeys + n_ctx), F32)],
        compiler_params=_cparams(("parallel", "parallel"), 48),
        name="neighbourhood_attention",
    )(nq, nk, nv, kc, vc, bias)


def _ctx_attn_kernel(q_ref, k_ref, v_ref, o_ref):
    q = q_ref[...]
    k = k_ref[...]
    v = v_ref[...]
    lane = lax.broadcasted_iota(jnp.int32, q.shape, 1)
    head0 = lane < NA_DH
    zero = jnp.zeros_like(q)
    outs = []
    for hh in range(2):
        qh = jnp.where(head0, q, zero) if hh == 0 else jnp.where(head0, zero, q)
        outs.append(_softmax_pv([_dot_nt(qh, k)], [v]))
    o_ref[...] = jnp.where(head0, outs[0], outs[1]).astype(BF16)


def _context_attention(q, k, v):
    b, n_ctx, hw = q.shape
    blk = pl.BlockSpec((None, n_ctx, V7X_LANES), lambda bi, pi: (bi, 0, pi))
    return pl.pallas_call(
        _ctx_attn_kernel,
        grid=(b, NA_HEADS // 2),
        in_specs=[blk, blk, blk],
        out_specs=blk,
        out_shape=jax.ShapeDtypeStruct((b, n_ctx, hw), BF16),
        compiler_params=_cparams(("parallel", "parallel"), 32),
        name="context_attention",
    )(q, k, v)


def _dft_rows_kernel(zre_ref, zim_ref, w_ref, are_ref, aim_ref):
    r = zre_ref.shape[0]
    z = jnp.concatenate([zre_ref[...].astype(BF16), zim_ref[...].astype(BF16)], axis=0)
    a = _dot(w_ref[...], z)
    are_ref[...] = a[:r]
    aim_ref[...] = a[r:]


def _dft_rows(zre, zim, w2):
    b, r, wc = zre.shape
    cb = min(4096, wc)
    blk = pl.BlockSpec((None, r, cb), lambda bi, ci: (bi, 0, ci))
    return pl.pallas_call(
        _dft_rows_kernel,
        grid=(b, wc // cb),
        in_specs=[blk, blk, _resident((2 * r, 2 * r))],
        out_specs=[blk, blk],
        out_shape=[jax.ShapeDtypeStruct((b, r, wc), F32)] * 2,
        compiler_params=_cparams(("parallel", "parallel"), 40),
        name="dft_rows",
    )(zre, zim, w2)


def _dft_cols_kernel(are_ref, aim_ref, twc_ref, tws_ref, km_ref, wo_ref, o_ref, *, scale, ch):
    w = are_ref.shape[1] // ch
    ar = jnp.concatenate([are_ref[:, c * ch:(c + 1) * ch] for c in range(w)], axis=0)
    ai = jnp.concatenate([aim_ref[:, c * ch:(c + 1) * ch] for c in range(w)], axis=0)
    reps = ch // V7X_LANES
    tc = jnp.concatenate([twc_ref[...]] * reps, axis=1)
    ts = jnp.concatenate([tws_ref[...]] * reps, axis=1)
    a = jnp.concatenate([(ar * tc - ai * ts).astype(BF16), (ar * ts + ai * tc).astype(BF16)], axis=0)
    y = _dot(km_ref[...], a) * scale
    out = _dot(y.astype(BF16), wo_ref[...])
    o_ref[...] = out.reshape(o_ref.shape)


def _dft_cols(a_re, a_im, twc, tws, kmat, w_out, *, layer, scale):
    b, r, wc = a_re.shape
    w = GRID_W
    ch = wc // w
    d = w_out.shape[2]
    jb = FOU_K1_BLOCK
    blk = pl.BlockSpec((None, jb, wc), lambda bi, ji: (bi, ji, 0))
    tw = pl.BlockSpec((jb * w, V7X_LANES), lambda bi, ji: (ji, 0))
    kern = functools.partial(_dft_cols_kernel, scale=scale, ch=ch)
    return pl.pallas_call(
        kern,
        grid=(b, r // jb),
        in_specs=[blk, blk, tw, tw, _resident(kmat.shape), _resident((ch, d), layer)],
        out_specs=pl.BlockSpec((None, w, jb, d), lambda bi, ji: (bi, 0, ji, 0)),
        out_shape=jax.ShapeDtypeStruct((b, w, r, d), F32),
        compiler_params=_cparams(("parallel", "parallel"), 32),
        name="dft_cols",
    )(a_re, a_im, twc, tws, kmat, w_out)


def _ctx_fourier_kernel(zre_ref, zim_ref, cm_ref, wo_ref, o_ref, *, scale):
    z = jnp.concatenate([zre_ref[...], zim_ref[...]], axis=0)
    y = _dot(cm_ref[...], z) * scale
    o_ref[...] = _dot(y.astype(BF16), wo_ref[...])


def _ctx_fourier(zre, zim, cmat, w_out, *, layer, scale):
    b, n_ctx, ch = zre.shape
    d = w_out.shape[2]
    blk = pl.BlockSpec((None, n_ctx, ch), lambda bi: (bi, 0, 0))
    kern = functools.partial(_ctx_fourier_kernel, scale=scale)
    return pl.pallas_call(
        kern,
        grid=(b,),
        in_specs=[blk, blk, _resident(cmat.shape), _resident((ch, d), layer)],
        out_specs=pl.BlockSpec((None, n_ctx, d), lambda bi: (bi, 0, 0)),
        out_shape=jax.ShapeDtypeStruct((b, n_ctx, d), F32),
        compiler_params=_cparams(("parallel",), 32),
        name="ctx_fourier",
    )(zre, zim, cmat, w_out)


def _merge_kernel(x_ref, mods_ref, ret_ref, na_ref, fou_ref, gt_ref, wr_ref, wn_ref, wo_ref, ln_ref,
                  o_ref, *, alpha):
    d = x_ref.shape[1]
    m = mods_ref[...]
    ln = ln_ref[...]
    for rows in _row_splits(x_ref.shape[0]):
        y_ret = _dot(ret_ref[rows, :], wr_ref[...])
        y_na = _dot(na_ref[rows, :], wn_ref[...])
        mix = (gt_ref[rows, 0:d].astype(F32) * y_ret + gt_ref[rows, d:2 * d].astype(F32) * y_na
               + gt_ref[rows, 2 * d:3 * d].astype(F32) * fou_ref[rows, :])
        y = _dot(mix.astype(BF16), wo_ref[...])
        z = alpha * x_ref[rows, :] + m[5:6] * y
        o_ref[rows, :] = _ln_rows(z) * ln[1:2] + ln[4:5]


def _merge(x2, mods, mods_row, ret, na, fou, gt, wr, wn, wo, ln, *, layer, tm, alpha):
    t, d = x2.shape
    kern = functools.partial(_merge_kernel, alpha=alpha)

    def rows(wd):
        return pl.BlockSpec((tm, wd), lambda i: (i, 0))
    return pl.pallas_call(
        kern,
        grid=(t // tm,),
        in_specs=[rows(d), pl.BlockSpec((None, None, 9, d), lambda i: (layer, mods_row(i), 0, 0)),
                  rows(ret.shape[1]), rows(na.shape[1]), rows(d), rows(gt.shape[1]),
                  _resident(wr.shape[1:], layer), _resident(wn.shape[1:], layer),
                  _resident(wo.shape[1:], layer), _resident((6, d), layer)],
        out_specs=rows(d),
        out_shape=jax.ShapeDtypeStruct((t, d), F32),
        compiler_params=_cparams(("parallel",), 60 if tm > TOKEN_TILE else 48),
        name="merge",
    )(x2, mods, ret, na, fou, gt, wr, wn, wo, ln)


def _rope_tables(n):
    half = RET_DK // 4
    t = np.arange(n)
    row, col = t // GRID_W, t % GRID_W
    freqs = ROPE_BASE ** (-np.arange(half, dtype=np.float64) / half)
    ang_r = row.astype(np.float64)[:, None] * freqs[None, :]
    ang_c = col.astype(np.float64)[:, None] * freqs[None, :]
    cos = np.concatenate([np.cos(ang_r)] * 2 + [np.cos(ang_c)] * 2, axis=1)
    sin = np.concatenate([-np.sin(ang_r), np.sin(ang_r), -np.sin(ang_c), np.sin(ang_c)], axis=1)
    return jnp.asarray(cos, F32), jnp.asarray(sin, F32)


def _dft_tables(n):
    w = GRID_W
    r = n // w
    jb = FOU_K1_BLOCK
    two_pi = 2.0 * np.pi
    ch = np.arange(FOU_DG)
    ang = two_pi * ((ch[:, None] * ch[None, :]) % FOU_DG) / FOU_DG
    cs = np.concatenate([np.cos(ang), np.sin(ang)], axis=1)
    k1 = np.arange(r)
    ang = two_pi * ((k1[:, None] * k1[None, :]) % r) / r
    w2 = np.block([[np.cos(ang), -np.sin(ang)], [np.sin(ang), np.cos(ang)]])
    c = np.arange(w)
    k1_blocks = k1.reshape(r // jb, 1, jb)
    ang = two_pi * (k1_blocks * c[None, :, None]).reshape(-1) / n
    twc = np.repeat(np.cos(ang)[:, None], V7X_LANES, axis=1)
    tws = np.repeat(np.sin(ang)[:, None], V7X_LANES, axis=1)
    ang = two_pi * ((c[:, None] * c[None, :]) % w) / w
    eye = np.eye(jb)
    kc = np.einsum("kc,ji->kjci", np.cos(ang), eye).reshape(w * jb, w * jb)
    ks = np.einsum("kc,ji->kjci", np.sin(ang), eye).reshape(w * jb, w * jb)
    kmat = np.concatenate([kc, -ks], axis=1)
    as_bf16 = lambda a: jnp.asarray(a, F32).astype(BF16)
    return (as_bf16(cs), as_bf16(w2), jnp.asarray(twc, F32), jnp.asarray(tws, F32), as_bf16(kmat))


def _ctx_dft_matrix(n_ctx):
    m = np.arange(n_ctx)
    ang = 2.0 * np.pi * ((m[:, None] * m[None, :]) % n_ctx) / n_ctx
    return jnp.asarray(np.concatenate([np.cos(ang), -np.sin(ang)], axis=1), F32).astype(BF16)


def kernel(x, c, ctx, c_ctx, ada_w, ada_b, ln_w, ln_b, ffa_w1, ffa_w3, ffa_w2, mix_w_in,
           ret_decay_logit, ret_gn_w, ret_gn_b, ret_w_out, na_rpb, na_w_out, fou_w_out, mix_w_o,
           ffb_w1, ffb_w3, ffb_w2):
    b, n, d = x.shape
    n_ctx = ctx.shape[1]
    depth = ada_w.shape[0]
    rows = n // GRID_W
    assert n % TOKEN_TILE == 0 and rows % (2 * NA_GROUP_ROWS) == 0 and rows >= NA_KEY_ROWS
    assert rows % FOU_K1_BLOCK == 0 and b + 1 <= 8
    alpha = (2 * depth) ** 0.25
    tm = TOKEN_TILE
    tm_ffn = FFN_TILE if n % FFN_TILE == 0 else TOKEN_TILE
    tiles_per_seq = n // tm
    lat_row = lambda i: i // tiles_per_seq
    ffn_row = lambda i: i // (n // tm_ffn)
    ctx_row = lambda i: b
    lat_rope = lambda i: i % tiles_per_seq
    ctx_rope = lambda i: 0
    t_ctx = b * n_ctx
    tc = TOKEN_TILE if t_ctx % TOKEN_TILE == 0 else n_ctx
    tc_ffn = FFN_TILE if t_ctx % FFN_TILE == 0 else tc

    cos_x, sin_x = _rope_tables(n)
    cos_c = jnp.ones((tc, V7X_LANES), F32)
    sin_c = jnp.zeros((tc, V7X_LANES), F32)
    cs, w2, twc, tws, kmat = _dft_tables(n)
    cmat_ctx = _ctx_dft_matrix(n_ctx)
    fou_scale_x = 1.0 / math.sqrt(n * FOU_DG)
    fou_scale_c = 1.0 / math.sqrt(n_ctx * FOU_DG)

    cvec = jnp.concatenate([c, c_ctx[None, :], jnp.zeros((8 - b - 1, d), F32)], axis=0)
    mods_all = _ada_mods(cvec, ada_w, ada_b).reshape(depth, 8, 9, d)
    ln_all = jnp.concatenate([ln_w, ln_b], axis=1)
    to_bf16 = lambda a: a.astype(BF16)
    ffa = [to_bf16(a) for a in (ffa_w1, ffa_w3, ffa_w2)]
    ffb = [to_bf16(a) for a in (ffb_w1, ffb_w3, ffb_w2)]
    w_in, wr, wn, wf, wo = [to_bf16(a) for a in (mix_w_in, ret_w_out, na_w_out, fou_w_out, mix_w_o)]
    lg_all = jnp.broadcast_to(
        jnp.transpose(ret_decay_logit, (0, 2, 1))[:, :, :, None, None],
        (depth, RET_HEADS, 2, 8, V7X_LANES)).astype(F32)
    gn_all = jnp.stack([ret_gn_w.reshape(depth, RET_HEADS, RET_DV),
                        ret_gn_b.reshape(depth, RET_HEADS, RET_DV)], axis=2)

    xs = x.reshape(b * n, d)
    cx = ctx.reshape(b * n_ctx, d)
    mods, ln = mods_all, ln_all
    for l in range(depth):
        last = l == depth - 1
        xs = _ffn(xs, mods, ffn_row, *ffa, ln, layer=l, tm=tm_ffn, mod_base=0, ln_row=0, alpha=alpha)
        cx = _ffn(cx, mods, ctx_row, *ffa, ln, layer=l, tm=tc_ffn, mod_base=0, ln_row=0, alpha=alpha)

        px = _proj(xs, mods, lat_row, w_in, cos_x, sin_x, lat_rope, cs, layer=l, tm=tm,
                   grid_rows_out=True)
        pc = _proj(cx, mods, ctx_row, w_in, cos_c, sin_c, ctx_rope, cs, layer=l, tm=tc,
                   grid_rows_out=False)
        q_x, k_x, v_x, g_x, nq_x, nk_x, nv_x = [a.reshape(b, n, -1) for a in px[:7]]
        gt_x = px[9]
        q_c, k_c, v_c, g_c, nq_c, nk_c, nv_c, zre_c, zim_c, gt_c = [a.reshape(b, n_ctx, -1) for a in pc]

        ret_x = _retention(lg_all, q_x, k_x, v_x, g_x, gn_all, k_c, v_c, layer=l)
        bias = _na_bias_tables(na_rpb[l])
        na_x = _neighbourhood_attention(nq_x, nk_x, nv_x, nk_c, nv_c, bias)
        a_re, a_im = _dft_rows(px[7].reshape(b, rows, -1), px[8].reshape(b, rows, -1), w2)
        fou_x = _dft_cols(a_re, a_im, twc, tws, kmat, wf, layer=l, scale=fou_scale_x)
        xs = _merge(xs, mods, ffn_row, ret_x.reshape(b * n, -1), na_x.reshape(b * n, -1),
                    fou_x.reshape(b * n, d), gt_x, wr, wn, wo, ln,
                    layer=l, tm=tm_ffn, alpha=alpha)
        xs = _ffn(xs, mods, ffn_row, *ffb, ln, layer=l, tm=tm_ffn, mod_base=6, ln_row=2, alpha=alpha)
        if not last:
            ret_c = _retention(lg_all, q_c, k_c, v_c, g_c, gn_all, layer=l)
            na_c = _context_attention(nq_c, nk_c, nv_c)
            fou_c = _ctx_fourier(zre_c, zim_c, cmat_ctx, wf, layer=l, scale=fou_scale_c)
            cx = _merge(cx, mods, ctx_row, ret_c.reshape(b * n_ctx, -1), na_c.reshape(b * n_ctx, -1),
                        fou_c.reshape(b * n_ctx, d), gt_c.reshape(b * n_ctx, -1), wr, wn, wo, ln,
                        layer=l, tm=tc_ffn, alpha=alpha)
            cx = _ffn(cx, mods, ctx_row, *ffb, ln, layer=l, tm=tc_ffn, mod_base=6, ln_row=2,
                      alpha=alpha)
    return xs.reshape(b, n, d)
```

```python
import functools
import math

import numpy as np
import jax
import jax.numpy as jnp
from jax import lax
from jax.experimental import pallas as pl
from jax.experimental.pallas import tpu as pltpu

F32 = jnp.float32
BF16 = jnp.bfloat16

GRID_W = 64
RET_HEADS = 4
RET_DK = 128
RET_DV = 256
NA_HEADS = 8
NA_DH = 64
NA_KH = 8
NA_KW = 16
FOU_GROUPS = 4
FOU_DG = 128
N_BRANCH = 3
ROPE_BASE = 10000.0
LN_EPS = 1e-6

V7X_VMEM_BYTES = 64 * 1024 * 1024
V7X_LANES = 128
V7X_SUBLANES = 8

TOKEN_TILE = 512
RET_CHUNK = 256
FFN_TILE = 1024
TOKEN_SPLIT = 4
SUBTILE_MIN = 256
RET_UNROLL = 8
NA_GROUP_ROWS = 4
NA_KEY_ROWS = NA_GROUP_ROWS + NA_KH
FOU_K1_BLOCK = V7X_SUBLANES
FOU_A_ROWS = 2 * V7X_SUBLANES
NEG_BIG = -1e30
LOG2_E = math.log2(math.e)


def _cparams(sem, vmem_mb):
    return pltpu.CompilerParams(dimension_semantics=sem,
                                vmem_limit_bytes=int(vmem_mb * 1024 * 1024))


def _resident(shape, layer=None):
    nd = len(shape)
    if layer is None:
        return pl.BlockSpec(shape, lambda *_: (0,) * nd, pipeline_mode=pl.Buffered(1))
    return pl.BlockSpec((None,) + tuple(shape), lambda *_: (layer,) + (0,) * nd,
                        pipeline_mode=pl.Buffered(1))


def _row_splits(tm):
    n = max(1, min(TOKEN_SPLIT, tm // SUBTILE_MIN))
    step = tm // n
    return [slice(s * step, (s + 1) * step) for s in range(n)]


def _ln_rows(x):
    mu = jnp.mean(x, axis=-1, keepdims=True)
    xc = x - mu
    var = jnp.mean(xc * xc, axis=-1, keepdims=True)
    return xc * lax.rsqrt(var + LN_EPS)


def _sigmoid(x):
    return 1.0 / (1.0 + jnp.exp(-x))


def _dot(a, b):
    return jnp.dot(a, b, preferred_element_type=F32)


def _dot_nt(a, b):
    return lax.dot_general(a, b, (((1,), (1,)), ((), ())), preferred_element_type=F32)


def _dot_tn(a, b):
    return lax.dot_general(a, b, (((0,), (0,)), ((), ())), preferred_element_type=F32)


def _mods_kernel(c_ref, w_ref, b_ref, o_ref):
    c = c_ref[...]
    h = (c * _sigmoid(c)).astype(BF16)
    o_ref[0] = _dot(h, w_ref[0].astype(BF16)) + b_ref[0]


def _ada_mods(cvec, ada_w, ada_b):
    depth, d, nine_d = ada_w.shape
    wb = 3 * d
    nblk = nine_d // wb
    return pl.pallas_call(
        _mods_kernel,
        grid=(depth, nblk),
        in_specs=[pl.BlockSpec((8, d), lambda l, j: (0, 0)),
                  pl.BlockSpec((1, d, wb), lambda l, j: (l, 0, j)),
                  pl.BlockSpec((1, 1, wb), lambda l, j: (l, 0, j))],
        out_specs=pl.BlockSpec((1, 8, wb), lambda l, j: (l, 0, j)),
        out_shape=jax.ShapeDtypeStruct((depth, 8, nine_d), F32),
        compiler_params=_cparams(("parallel", "parallel"), 40),
        name="ada_mods",
    )(cvec, ada_w, ada_b.reshape(depth, 1, nine_d))


def _ffn_kernel(x_ref, mods_ref, w1_ref, w3_ref, w2_ref, ln_ref, o_ref, *, mod_base, ln_row, alpha):
    m = mods_ref[...]
    shift = m[mod_base:mod_base + 1]
    scale = m[mod_base + 1:mod_base + 2]
    gate = m[mod_base + 2:mod_base + 3]
    ln = ln_ref[...]
    for rows in _row_splits(x_ref.shape[0]):
        x = x_ref[rows, :]
        h = (_ln_rows(x) * (1.0 + scale) + shift).astype(BF16)
        a = _dot(h, w1_ref[...])
        b = _dot(h, w3_ref[...])
        g = (a * _sigmoid(a) * b).astype(BF16)
        y = _dot(g, w2_ref[...])
        z = alpha * x + 0.5 * gate * y
        o_ref[rows, :] = _ln_rows(z) * ln[ln_row:ln_row + 1] + ln[3 + ln_row:4 + ln_row]


def _ffn(x2, mods, mods_row, w1, w3, w2, ln, *, layer, tm, mod_base, ln_row, alpha):
    t, d = x2.shape
    dff = w1.shape[2]
    kern = functools.partial(_ffn_kernel, mod_base=mod_base, ln_row=ln_row, alpha=alpha)
    return pl.pallas_call(
        kern,
        grid=(t // tm,),
        in_specs=[pl.BlockSpec((tm, d), lambda i: (i, 0)),
                  pl.BlockSpec((None, None, 9, d), lambda i: (layer, mods_row(i), 0, 0)),
                  _resident((d, dff), layer), _resident((d, dff), layer), _resident((dff, d), layer),
                  _resident((6, d), layer)],
        out_specs=pl.BlockSpec((tm, d), lambda i: (i, 0)),
        out_shape=jax.ShapeDtypeStruct((t, d), F32),
        compiler_params=_cparams(("parallel",), 62 if tm > TOKEN_TILE else 56),
        name="ffn",
    )(x2, mods, w1, w3, w2, ln)


def _rope(t, cos, sin_signed, first_half):
    swapped = jnp.where(first_half, pltpu.roll(t, 96, 1), pltpu.roll(t, 32, 1))
    return t * cos + swapped * sin_signed


def _proj_kernel(x_ref, mods_ref, w_ref, cos_ref, sin_ref, cs_ref, *rest,
                 q_scale, nq_scale, grid_rows_out):
    extra = rest[:-10]
    q_ref, k_ref, v_ref, g_ref, nq_ref, nk_ref, nv_ref, zre_ref, zim_ref, gt_ref = rest[-10:]
    x = x_ref[...]
    m = mods_ref[...]
    h = (_ln_rows(x) * (1.0 + m[4:5]) + m[3:4]).astype(BF16)
    tm = x.shape[0]
    cw = 512

    def seg(j):
        return _dot(h, w_ref[:, j * cw:(j + 1) * cw])

    cos = cos_ref[...]
    sin = sin_ref[...]
    lane = lax.broadcasted_iota(jnp.int32, (tm, V7X_LANES), 1)
    first_half = (lane & 32) == 0

    cs = cs_ref[...]
    u = seg(9).astype(BF16)
    r = seg(0) * q_scale
    for hh in range(RET_HEADS):
        sl = slice(hh * RET_DK, (hh + 1) * RET_DK)
        q_ref[:, sl] = _rope(r[:, sl], cos, sin, first_half).astype(BF16)
    if grid_rows_out:
        perm_ref, = extra
        u = _dot(perm_ref[...], u).astype(BF16)
    r = seg(1)
    for hh in range(RET_HEADS):
        sl = slice(hh * RET_DK, (hh + 1) * RET_DK)
        k_ref[:, sl] = _rope(r[:, sl], cos, sin, first_half).astype(BF16)
    n_r = tm // GRID_W

    def fourier(gi):
        sl = slice(gi * FOU_DG, (gi + 1) * FOU_DG)
        z = _dot(u[:, sl], cs)
        if grid_rows_out:
            for c in range(GRID_W):
                dst = slice((c * FOU_GROUPS + gi) * FOU_DG, (c * FOU_GROUPS + gi + 1) * FOU_DG)
                zre_ref[:, dst] = z[c * n_r:(c + 1) * n_r, :FOU_DG]
                zim_ref[:, dst] = z[c * n_r:(c + 1) * n_r, FOU_DG:]
        else:
            zre_ref[:, sl] = z[:, :FOU_DG].astype(BF16)
            zim_ref[:, sl] = z[:, FOU_DG:].astype(BF16)

    for j in range(2):
        r = seg(4 + j)
        g_ref[:, j * cw:(j + 1) * cw] = (r * _sigmoid(r)).astype(BF16)
        fourier(j)
    for j in range(6):
        gt_ref[:, j * cw:(j + 1) * cw] = _sigmoid(seg(10 + j)).astype(BF16)
        if j < FOU_GROUPS - 2:
            fourier(2 + j)
    nq_ref[...] = (seg(6) * nq_scale).astype(BF16)
    for j in range(2):
        v_ref[:, j * cw:(j + 1) * cw] = seg(2 + j).astype(BF16)
    nk_ref[...] = seg(7).astype(BF16)
    nv_ref[...] = seg(8).astype(BF16)


def _proj(x2, mods, mods_row, w_in, cos, sin, rope_row, cs, *, layer, tm, grid_rows_out):
    t, d = x2.shape
    d_in = w_in.shape[2]
    ch = FOU_GROUPS * FOU_DG
    widths = (512, 512, 1024, 1024, 512, 512, 512, ch, ch, 3072)
    kern = functools.partial(_proj_kernel, q_scale=RET_DK ** -0.5, nq_scale=NA_DH ** -0.5 * LOG2_E,
                             grid_rows_out=grid_rows_out)
    out_specs = [pl.BlockSpec((tm, w), lambda i: (i, 0)) for w in widths]
    out_shape = [jax.ShapeDtypeStruct((t, w), BF16) for w in widths]
    in_specs = [pl.BlockSpec((tm, d), lambda i: (i, 0)),
                pl.BlockSpec((None, None, 9, d), lambda i: (layer, mods_row(i), 0, 0)),
                _resident((d, d_in), layer),
                pl.BlockSpec((tm, V7X_LANES), lambda i: (rope_row(i), 0)),
                pl.BlockSpec((tm, V7X_LANES), lambda i: (rope_row(i), 0)),
                _resident((FOU_DG, 2 * FOU_DG))]
    args = [x2, mods, w_in, cos, sin, cs]
    if grid_rows_out:
        n_r = tm // GRID_W
        for o in (7, 8):
            out_specs[o] = pl.BlockSpec((n_r, GRID_W * ch), lambda i: (i, 0))
            out_shape[o] = jax.ShapeDtypeStruct((t // GRID_W, GRID_W * ch), F32)
        src = (np.arange(n_r)[None, :] * GRID_W + np.arange(GRID_W)[:, None]).reshape(-1)
        perm = np.zeros((tm, tm), np.float32)
        perm[np.arange(tm), src] = 1.0
        in_specs.append(_resident((tm, tm)))
        args.append(jnp.asarray(perm).astype(BF16))
    return pl.pallas_call(
        kern,
        grid=(t // tm,),
        in_specs=in_specs,
        out_specs=out_specs,
        out_shape=out_shape,
        compiler_params=_cparams(("parallel",), 56),
        name="mixer_in_proj",
    )(*args)


def _log_sigmoid(x):
    return jnp.minimum(x, 0.0) - jnp.log(1.0 + jnp.exp(-jnp.abs(x)))


def _kv_update(k, v, lf, lb, idx):
    n = k.shape[0]
    kf = k.astype(F32)
    kk = jnp.concatenate([(kf * jnp.exp(lf * (n - 1.0 - idx))).astype(BF16),
                          (kf * jnp.exp(lb * idx)).astype(BF16)], axis=1)
    return _dot_tn(kk, v)


def _ret_kernel(*refs, n_chunks, chunk, with_prefix):
    if with_prefix:
        lg_ref, q_ref, k_ref, v_ref, g_ref, gn_ref, kc_ref, vc_ref, o_ref, u_ref, s_ref = refs
    else:
        lg_ref, q_ref, k_ref, v_ref, g_ref, gn_ref, o_ref, u_ref, s_ref = refs
    c = chunk
    dk, dv = RET_DK, RET_DV
    ls = _log_sigmoid(lg_ref[...])
    lf = ls[0, 0:1, :]
    lb = ls[1, 0:1, :]
    lf_v = jnp.concatenate([lf] * (dv // V7X_LANES), axis=1)
    lb_v = jnp.concatenate([lb] * (dv // V7X_LANES), axis=1)
    lf_c = jnp.concatenate([lf] * (c // V7X_LANES), axis=1)
    lb_c = jnp.concatenate([lb] * (c // V7X_LANES), axis=1)
    idx = lax.broadcasted_iota(jnp.int32, (c, V7X_LANES), 0).astype(F32)

    def upd(i, carry):
        r0 = pl.multiple_of(i * c, c)
        u_ref[i] = _kv_update(k_ref[pl.ds(r0, c), :], v_ref[pl.ds(r0, c), :], lf, lb, idx)
        return carry
    lax.fori_loop(0, n_chunks, upd, 0, unroll=min(RET_UNROLL, n_chunks))

    if with_prefix:
        n_ctx = kc_ref.shape[0]
        idx_c = lax.broadcasted_iota(jnp.int32, (n_ctx, V7X_LANES), 0).astype(F32)
        u0 = _kv_update(kc_ref[...], vc_ref[...], lf, lb, idx_c)
        sf0, sb0 = u0[:dk], u0[dk:]
    else:
        sf0 = jnp.zeros((dk, dv), F32)
        sb0 = sf0
    gf_c = jnp.exp(lf_v * float(c))
    gb_c = jnp.exp(lb_v * float(c))

    def scan_f(i, sf):
        s_ref[i, :dk, :] = sf.astype(BF16)
        return sf * gf_c + u_ref[i, :dk, :]
    lax.fori_loop(0, n_chunks, scan_f, sf0)

    def scan_b(t, sb):
        i = n_chunks - 1 - t
        s_ref[i, dk:, :] = sb.astype(BF16)
        return sb * gb_c + u_ref[i, dk:, :]
    lax.fori_loop(0, n_chunks, scan_b, sb0)

    pr = (lax.broadcasted_iota(jnp.int32, (c, c), 0)
          - lax.broadcasted_iota(jnp.int32, (c, c), 1)).astype(F32)
    dmat = jnp.where(pr >= 0.0, jnp.exp(lf_c * jnp.maximum(pr, 0.0)),
                     jnp.exp(lb_c * jnp.maximum(-pr, 0.0)))
    qdf = jnp.exp(lf * (idx + 1.0))
    qdb = jnp.exp(lb * (float(c) - idx))
    gn = gn_ref[...]
    gn_w, gn_b = gn[0:1], gn[1:2]

    def out(i, carry):
        r0 = pl.multiple_of(i * c, c)
        q = q_ref[pl.ds(r0, c), :]
        k = k_ref[pl.ds(r0, c), :]
        v = v_ref[pl.ds(r0, c), :]
        inner = (_dot_nt(q, k) * dmat).astype(BF16)
        qf = q.astype(F32)
        qq = jnp.concatenate([(qf * qdf).astype(BF16), (qf * qdb).astype(BF16)], axis=1)
        o = _dot(inner, v) + _dot(qq, s_ref[i])
        o = _ln_rows(o) * gn_w + gn_b
        o_ref[pl.ds(r0, c), :] = (o * g_ref[pl.ds(r0, c), :].astype(F32)).astype(BF16)
        return carry
    lax.fori_loop(0, n_chunks, out, 0, unroll=min(RET_UNROLL, n_chunks))


def _retention(lg, q, k, v, g, gn, kc=None, vc=None, *, layer):
    b, n, _ = q.shape
    c = min(RET_CHUNK, n)
    nc = n // c
    with_prefix = kc is not None
    kern = functools.partial(_ret_kernel, n_chunks=nc, chunk=c, with_prefix=with_prefix)
    in_specs = [pl.BlockSpec((None, None, 2, 8, V7X_LANES), lambda bi, hi: (layer, hi, 0, 0, 0)),
                pl.BlockSpec((None, n, RET_DK), lambda bi, hi: (bi, 0, hi)),
                pl.BlockSpec((None, n, RET_DK), lambda bi, hi: (bi, 0, hi)),
                pl.BlockSpec((None, n, RET_DV), lambda bi, hi: (bi, 0, hi)),
                pl.BlockSpec((None, n, RET_DV), lambda bi, hi: (bi, 0, hi)),
                pl.BlockSpec((None, None, 2, RET_DV), lambda bi, hi: (layer, hi, 0, 0))]
    args = [lg, q, k, v, g, gn]
    if with_prefix:
        n_ctx = kc.shape[1]
        in_specs += [pl.BlockSpec((None, n_ctx, RET_DK), lambda bi, hi: (bi, 0, hi)),
                     pl.BlockSpec((None, n_ctx, RET_DV), lambda bi, hi: (bi, 0, hi))]
        args += [kc, vc]
    return pl.pallas_call(
        kern,
        grid=(b, RET_HEADS),
        in_specs=in_specs,
        out_specs=pl.BlockSpec((None, n, RET_DV), lambda bi, hi: (bi, 0, hi)),
        out_shape=jax.ShapeDtypeStruct((b, n, RET_HEADS * RET_DV), BF16),
        scratch_shapes=[pltpu.VMEM((nc, 2 * RET_DK, RET_DV), F32),
                        pltpu.VMEM((nc, 2 * RET_DK, RET_DV), BF16)],
        compiler_params=_cparams(("parallel", "parallel"), 56),
        name="retention",
    )(*args)


def _softmax_pv(s_parts, v_parts):
    m = None
    for s in s_parts:
        mi = jnp.max(s, axis=-1, keepdims=True)
        m = mi if m is None else jnp.maximum(m, mi)
    l = None
    o = None
    for s, v in zip(s_parts, v_parts):
        p = jnp.exp2(s - m)
        li = jnp.sum(p, axis=-1, keepdims=True)
        oi = _dot(p.astype(BF16), v)
        l = li if l is None else l + li
        o = oi if o is None else o + oi
    return o / l


def _na_kernel(q_ref, k_ref, v_ref, kc_ref, vc_ref, bias_ref, o_ref, s_ref, *, n_groups, key_row0_max):
    gq = NA_GROUP_ROWS * GRID_W
    nk = NA_KEY_ROWS * GRID_W
    lane = lax.broadcasted_iota(jnp.int32, (gq, V7X_LANES), 1)
    head0 = lane < NA_DH
    kc = kc_ref[...]
    vc = vc_ref[...]

    def key_start(g):
        kb = jnp.clip(g * NA_GROUP_ROWS - NA_KH // 2, 0, key_row0_max)
        return pl.multiple_of(kb * GRID_W, GRID_W)

    def scores(g, slot):
        q0 = pl.multiple_of(g * gq, gq)
        cfg = jnp.where(g == 0, 0, jnp.where(g == n_groups - 1, 2, 1))
        q = q_ref[pl.ds(q0, gq), :]
        kl = k_ref[pl.ds(key_start(g), nk), :]
        zero = jnp.zeros_like(q)
        for hh in range(2):
            qh = jnp.where(head0, q, zero) if hh == 0 else jnp.where(head0, zero, q)
            s_ref[slot, hh, :, :nk] = _dot_nt(qh, kl) + bias_ref[cfg, hh]
            s_ref[slot, hh, :, nk:] = _dot_nt(qh, kc)

    def finish(g, slot):
        q0 = pl.multiple_of(g * gq, gq)
        vl = v_ref[pl.ds(key_start(g), nk), :]
        outs = [_softmax_pv([s_ref[slot, hh, :, :nk], s_ref[slot, hh, :, nk:]], [vl, vc])
                for hh in range(2)]
        o_ref[pl.ds(q0, gq), :] = jnp.where(head0, outs[0], outs[1]).astype(BF16)

    scores(0, 0)

    def pair(j, carry):
        g = 2 * j
        scores(g + 1, 1)
        finish(g, 0)
        scores(jnp.minimum(g + 2, n_groups - 1), 0)
        finish(g + 1, 1)
        return carry
    lax.fori_loop(0, n_groups // 2, pair, 0, unroll=2)


def _na_bias_tables(rpb):
    h = rpb.shape[0]
    w = GRID_W
    cols = np.arange(w)
    cs = np.clip(cols - NA_KW // 2, 0, w - NA_KW)
    kcol = np.arange(w)[None, :]
    colmask = (kcol >= cs[:, None]) & (kcol < cs[:, None] + NA_KW)
    pad = w - 1
    rpb_p = jnp.pad(rpb, ((0, 0), (0, 0), (pad, pad)))
    toep = jnp.stack([rpb_p[:, :, pad + NA_KW - 1 - c: pad + NA_KW - 1 - c + w] for c in range(w)],
                     axis=2)
    toep = jnp.where(colmask[None, None], toep * LOG2_E, NEG_BIG)
    n_dr = toep.shape[1]
    toep = jnp.concatenate([toep, jnp.full((h, 1, w, w), NEG_BIG, F32)], axis=1)
    dr = np.full((3, NA_GROUP_ROWS, NA_KEY_ROWS), n_dr, np.int32)
    for cfg in range(3):
        rel = NA_GROUP_ROWS * cfg
        for rr in range(NA_GROUP_ROWS):
            lo = (0, rr, NA_KEY_ROWS - NA_KH)[cfg]
            for i in range(lo, lo + NA_KH):
                dr[cfg, rr, i] = i - rel - rr + NA_KH - 1
    pairs = dr.reshape(3, NA_GROUP_ROWS, NA_KEY_ROWS // 2, 2)
    uniq = sorted({(int(a), int(c)) for a, c in pairs.reshape(-1, 2)})
    slot_of = {p: u for u, p in enumerate(uniq)}
    piece = np.array([[[slot_of[(int(a), int(c))] for a, c in row] for row in cfg] for cfg in pairs])
    left = jnp.take(toep, jnp.asarray([p[0] for p in uniq]), axis=1)
    right = jnp.take(toep, jnp.asarray([p[1] for p in uniq]), axis=1)
    pieces = jnp.concatenate([left, right], axis=-1)
    n_u = len(uniq)

    def assemble(p_ref, o_ref):
        for cfg in range(3):
            for rr in range(NA_GROUP_ROWS):
                for pc in range(NA_KEY_ROWS // 2):
                    o_ref[cfg, rr * w:(rr + 1) * w, pc * 2 * w:(pc + 1) * 2 * w] = (
                        p_ref[int(piece[cfg, rr, pc])])

    return pl.pallas_call(
        assemble,
        grid=(h,),
        in_specs=[pl.BlockSpec((None, n_u, w, 2 * w), lambda hi: (hi, 0, 0, 0))],
        out_specs=pl.BlockSpec((3, None, NA_GROUP_ROWS * w, NA_KEY_ROWS * w),
                               lambda hi: (0, hi, 0, 0)),
        out_shape=jax.ShapeDtypeStruct((3, h, NA_GROUP_ROWS * w, NA_KEY_ROWS * w), F32),
        compiler_params=_cparams(("parallel",), 32),
        name="na_bias_table",
    )(pieces)


def _neighbourhood_attention(nq, nk, nv, kc, vc, bias):
    b, n, hw = nq.shape
    rows = n // GRID_W
    n_ctx = kc.shape[1]
    n_groups = rows // NA_GROUP_ROWS
    pairs = NA_HEADS // 2
    gq = NA_GROUP_ROWS * GRID_W
    nkeys = NA_KEY_ROWS * GRID_W
    kern = functools.partial(_na_kernel, n_groups=n_groups, key_row0_max=rows - NA_KEY_ROWS)
    tok = pl.BlockSpec((None, n, V7X_LANES), lambda bi, pi: (bi, 0, pi))
    ctx = pl.BlockSpec((None, n_ctx, V7X_LANES), lambda bi, pi: (bi, 0, pi))
    return pl.pallas_call(
        kern,
        grid=(b, pairs),
        in_specs=[tok, tok, tok, ctx, ctx,
                  pl.BlockSpec((3, 2, gq, nkeys), lambda bi, pi: (0, pi, 0, 0))],
        out_specs=tok,
        out_shape=jax.ShapeDtypeStruct((b, n, hw), BF16),
        scratch_shapes=[pltpu.VMEM((2, 2, gq, nkeys + n_ctx), F32)],
        compiler_params=_cparams(("parallel", "parallel"), 48),
        name="neighbourhood_attention",
    )(nq, nk, nv, kc, vc, bias)


def _ctx_attn_kernel(q_ref, k_ref, v_ref, o_ref):
    q = q_ref[...]
    k = k_ref[...]
    v = v_ref[...]
    lane = lax.broadcasted_iota(jnp.int32, q.shape, 1)
    head0 = lane < NA_DH
    zero = jnp.zeros_like(q)
    outs = []
    for hh in range(2):
        qh = jnp.where(head0, q, zero) if hh == 0 else jnp.where(head0, zero, q)
        outs.append(_softmax_pv([_dot_nt(qh, k)], [v]))
    o_ref[...] = jnp.where(head0, outs[0], outs[1]).astype(BF16)


def _context_attention(q, k, v):
    b, n_ctx, hw = q.shape
    blk = pl.BlockSpec((None, n_ctx, V7X_LANES), lambda bi, pi: (bi, 0, pi))
    return pl.pallas_call(
        _ctx_attn_kernel,
        grid=(b, NA_HEADS // 2),
        in_specs=[blk, blk, blk],
        out_specs=blk,
        out_shape=jax.ShapeDtypeStruct((b, n_ctx, hw), BF16),
        compiler_params=_cparams(("parallel", "parallel"), 32),
        name="context_attention",
    )(q, k, v)


def _dft_rows_kernel(zre_ref, zim_ref, w_ref, are_ref, aim_ref):
    r = zre_ref.shape[0]
    z = jnp.concatenate([zre_ref[...].astype(BF16), zim_ref[...].astype(BF16)], axis=0)
    a = _dot(w_ref[...], z)
    are_ref[...] = a[:r].astype(BF16)
    aim_ref[...] = a[r:].astype(BF16)


def _dft_rows(zre, zim, w2):
    b, r, wc = zre.shape
    cb = min(4096, wc)
    blk = pl.BlockSpec((None, r, cb), lambda bi, ci: (bi, 0, ci))
    return pl.pallas_call(
        _dft_rows_kernel,
        grid=(b, wc // cb),
        in_specs=[blk, blk, _resident((2 * r, 2 * r))],
        out_specs=[blk, blk],
        out_shape=[jax.ShapeDtypeStruct((b, r, wc), BF16)] * 2,
        compiler_params=_cparams(("parallel", "parallel"), 40),
        name="dft_rows",
    )(zre, zim, w2)


def _dft_cols_kernel(are_ref, aim_ref, twc_ref, tws_ref, km_ref, wo_ref, o_ref, *, scale, ch, jb):
    w = are_ref.shape[1] // ch
    reps = ch // V7X_LANES
    slabs_re = [are_ref[:, c * ch:(c + 1) * ch].astype(F32) for c in range(w)]
    slabs_im = [aim_ref[:, c * ch:(c + 1) * ch].astype(F32) for c in range(w)]
    for hf in range(are_ref.shape[0] // jb):
        rs = slice(hf * jb, (hf + 1) * jb)
        ts_rows = slice(hf * jb * w, (hf + 1) * jb * w)
        ar = jnp.concatenate([s[rs] for s in slabs_re], axis=0)
        ai = jnp.concatenate([s[rs] for s in slabs_im], axis=0)
        tc = jnp.concatenate([twc_ref[ts_rows, :]] * reps, axis=1)
        ts = jnp.concatenate([tws_ref[ts_rows, :]] * reps, axis=1)
        a = jnp.concatenate([(ar * tc - ai * ts).astype(BF16), (ar * ts + ai * tc).astype(BF16)],
                            axis=0)
        y = _dot(km_ref[...], a) * scale
        out = _dot(y.astype(BF16), wo_ref[...])
        o_ref[:, rs, :] = out.reshape(w, jb, out.shape[1])


def _dft_cols(a_re, a_im, twc, tws, kmat, w_out, *, layer, scale):
    b, r, wc = a_re.shape
    w = GRID_W
    ch = wc // w
    d = w_out.shape[2]
    jb = FOU_K1_BLOCK
    ab = FOU_A_ROWS
    blk = pl.BlockSpec((None, ab, wc), lambda bi, ji: (bi, ji, 0))
    tw = pl.BlockSpec((ab * w, V7X_LANES), lambda bi, ji: (ji, 0))
    kern = functools.partial(_dft_cols_kernel, scale=scale, ch=ch, jb=jb)
    return pl.pallas_call(
        kern,
        grid=(b, r // ab),
        in_specs=[blk, blk, tw, tw, _resident(kmat.shape), _resident((ch, d), layer)],
        out_specs=pl.BlockSpec((None, w, ab, d), lambda bi, ji: (bi, 0, ji, 0)),
        out_shape=jax.ShapeDtypeStruct((b, w, r, d), F32),
        compiler_params=_cparams(("parallel", "parallel"), 40),
        name="dft_cols",
    )(a_re, a_im, twc, tws, kmat, w_out)


def _ctx_fourier_kernel(zre_ref, zim_ref, cm_ref, wo_ref, o_ref, *, scale):
    z = jnp.concatenate([zre_ref[...], zim_ref[...]], axis=0)
    y = _dot(cm_ref[...], z) * scale
    o_ref[...] = _dot(y.astype(BF16), wo_ref[...])


def _ctx_fourier(zre, zim, cmat, w_out, *, layer, scale):
    b, n_ctx, ch = zre.shape
    d = w_out.shape[2]
    blk = pl.BlockSpec((None, n_ctx, ch), lambda bi: (bi, 0, 0))
    kern = functools.partial(_ctx_fourier_kernel, scale=scale)
    return pl.pallas_call(
        kern,
        grid=(b,),
        in_specs=[blk, blk, _resident(cmat.shape), _resident((ch, d), layer)],
        out_specs=pl.BlockSpec((None, n_ctx, d), lambda bi: (bi, 0, 0)),
        out_shape=jax.ShapeDtypeStruct((b, n_ctx, d), F32),
        compiler_params=_cparams(("parallel",), 32),
        name="ctx_fourier",
    )(zre, zim, cmat, w_out)


def _merge_kernel(x_ref, mods_ref, ret_ref, na_ref, fou_ref, gt_ref, wr_ref, wn_ref, wo_ref, ln_ref,
                  o_ref, *, alpha):
    d = x_ref.shape[1]
    m = mods_ref[...]
    ln = ln_ref[...]
    for rows in _row_splits(x_ref.shape[0]):
        y_ret = _dot(ret_ref[rows, :], wr_ref[...])
        y_na = _dot(na_ref[rows, :], wn_ref[...])
        mix = (gt_ref[rows, 0:d].astype(F32) * y_ret + gt_ref[rows, d:2 * d].astype(F32) * y_na
               + gt_ref[rows, 2 * d:3 * d].astype(F32) * fou_ref[rows, :])
        y = _dot(mix.astype(BF16), wo_ref[...])
        z = alpha * x_ref[rows, :] + m[5:6] * y
        o_ref[rows, :] = _ln_rows(z) * ln[1:2] + ln[4:5]


def _merge(x2, mods, mods_row, ret, na, fou, gt, wr, wn, wo, ln, *, layer, tm, alpha):
    t, d = x2.shape
    kern = functools.partial(_merge_kernel, alpha=alpha)

    def rows(wd):
        return pl.BlockSpec((tm, wd), lambda i: (i, 0))
    return pl.pallas_call(
        kern,
        grid=(t // tm,),
        in_specs=[rows(d), pl.BlockSpec((None, None, 9, d), lambda i: (layer, mods_row(i), 0, 0)),
                  rows(ret.shape[1]), rows(na.shape[1]), rows(d), rows(gt.shape[1]),
                  _resident(wr.shape[1:], layer), _resident(wn.shape[1:], layer),
                  _resident(wo.shape[1:], layer), _resident((6, d), layer)],
        out_specs=rows(d),
        out_shape=jax.ShapeDtypeStruct((t, d), F32),
        compiler_params=_cparams(("parallel",), 60 if tm > TOKEN_TILE else 48),
        name="merge",
    )(x2, mods, ret, na, fou, gt, wr, wn, wo, ln)


def _rope_tables(n):
    half = RET_DK // 4
    t = np.arange(n)
    row, col = t // GRID_W, t % GRID_W
    freqs = ROPE_BASE ** (-np.arange(half, dtype=np.float64) / half)
    ang_r = row.astype(np.float64)[:, None] * freqs[None, :]
    ang_c = col.astype(np.float64)[:, None] * freqs[None, :]
    cos = np.concatenate([np.cos(ang_r)] * 2 + [np.cos(ang_c)] * 2, axis=1)
    sin = np.concatenate([-np.sin(ang_r), np.sin(ang_r), -np.sin(ang_c), np.sin(ang_c)], axis=1)
    return jnp.asarray(cos, F32), jnp.asarray(sin, F32)


def _dft_tables(n):
    w = GRID_W
    r = n // w
    jb = FOU_K1_BLOCK
    two_pi = 2.0 * np.pi
    ch = np.arange(FOU_DG)
    ang = two_pi * ((ch[:, None] * ch[None, :]) % FOU_DG) / FOU_DG
    cs = np.concatenate([np.cos(ang), np.sin(ang)], axis=1)
    k1 = np.arange(r)
    ang = two_pi * ((k1[:, None] * k1[None, :]) % r) / r
    w2 = np.block([[np.cos(ang), -np.sin(ang)], [np.sin(ang), np.cos(ang)]])
    c = np.arange(w)
    k1_blocks = k1.reshape(r // jb, 1, jb)
    ang = two_pi * (k1_blocks * c[None, :, None]).reshape(-1) / n
    twc = np.repeat(np.cos(ang)[:, None], V7X_LANES, axis=1)
    tws = np.repeat(np.sin(ang)[:, None], V7X_LANES, axis=1)
    ang = two_pi * ((c[:, None] * c[None, :]) % w) / w
    eye = np.eye(jb)
    kc = np.einsum("kc,ji->kjci", np.cos(ang), eye).reshape(w * jb, w * jb)
    ks = np.einsum("kc,ji->kjci", np.sin(ang), eye).reshape(w * jb, w * jb)
    kmat = np.concatenate([kc, -ks], axis=1)
    as_bf16 = lambda a: jnp.asarray(a, F32).astype(BF16)
    return (as_bf16(cs), as_bf16(w2), jnp.asarray(twc, F32), jnp.asarray(tws, F32), as_bf16(kmat))


def _ctx_dft_matrix(n_ctx):
    m = np.arange(n_ctx)
    ang = 2.0 * np.pi * ((m[:, None] * m[None, :]) % n_ctx) / n_ctx
    return jnp.asarray(np.concatenate([np.cos(ang), -np.sin(ang)], axis=1), F32).astype(BF16)


def kernel(x, c, ctx, c_ctx, ada_w, ada_b, ln_w, ln_b, ffa_w1, ffa_w3, ffa_w2, mix_w_in,
           ret_decay_logit, ret_gn_w, ret_gn_b, ret_w_out, na_rpb, na_w_out, fou_w_out, mix_w_o,
           ffb_w1, ffb_w3, ffb_w2):
    b, n, d = x.shape
    n_ctx = ctx.shape[1]
    depth = ada_w.shape[0]
    rows = n // GRID_W
    assert n % TOKEN_TILE == 0 and rows % (2 * NA_GROUP_ROWS) == 0 and rows >= NA_KEY_ROWS
    assert rows % FOU_A_ROWS == 0 and b + 1 <= 8
    alpha = (2 * depth) ** 0.25
    tm = TOKEN_TILE
    tm_ffn = FFN_TILE if n % FFN_TILE == 0 else TOKEN_TILE
    tiles_per_seq = n // tm
    lat_row = lambda i: i // tiles_per_seq
    ffn_row = lambda i: i // (n // tm_ffn)
    ctx_row = lambda i: b
    lat_rope = lambda i: i % tiles_per_seq
    ctx_rope = lambda i: 0
    t_ctx = b * n_ctx
    tc = TOKEN_TILE if t_ctx % TOKEN_TILE == 0 else n_ctx
    tc_ffn = FFN_TILE if t_ctx % FFN_TILE == 0 else tc

    cos_x, sin_x = _rope_tables(n)
    cos_c = jnp.ones((tc, V7X_LANES), F32)
    sin_c = jnp.zeros((tc, V7X_LANES), F32)
    cs, w2, twc, tws, kmat = _dft_tables(n)
    cmat_ctx = _ctx_dft_matrix(n_ctx)
    fou_scale_x = 1.0 / math.sqrt(n * FOU_DG)
    fou_scale_c = 1.0 / math.sqrt(n_ctx * FOU_DG)

    cvec = jnp.concatenate([c, c_ctx[None, :], jnp.zeros((8 - b - 1, d), F32)], axis=0)
    mods_all = _ada_mods(cvec, ada_w, ada_b).reshape(depth, 8, 9, d)
    ln_all = jnp.concatenate([ln_w, ln_b], axis=1)
    to_bf16 = lambda a: a.astype(BF16)
    ffa = [to_bf16(a) for a in (ffa_w1, ffa_w3, ffa_w2)]
    ffb = [to_bf16(a) for a in (ffb_w1, ffb_w3, ffb_w2)]
    w_in, wr, wn, wf, wo = [to_bf16(a) for a in (mix_w_in, ret_w_out, na_w_out, fou_w_out, mix_w_o)]
    lg_all = jnp.broadcast_to(
        jnp.transpose(ret_decay_logit, (0, 2, 1))[:, :, :, None, None],
        (depth, RET_HEADS, 2, 8, V7X_LANES)).astype(F32)
    gn_all = jnp.stack([ret_gn_w.reshape(depth, RET_HEADS, RET_DV),
                        ret_gn_b.reshape(depth, RET_HEADS, RET_DV)], axis=2)

    xs = x.reshape(b * n, d)
    cx = ctx.reshape(b * n_ctx, d)
    mods, ln = mods_all, ln_all
    for l in range(depth):
        last = l == depth - 1
        xs = _ffn(xs, mods, ffn_row, *ffa, ln, layer=l, tm=tm_ffn, mod_base=0, ln_row=0, alpha=alpha)
        cx = _ffn(cx, mods, ctx_row, *ffa, ln, layer=l, tm=tc_ffn, mod_base=0, ln_row=0, alpha=alpha)

        px = _proj(xs, mods, lat_row, w_in, cos_x, sin_x, lat_rope, cs, layer=l, tm=tm,
                   grid_rows_out=True)
        pc = _proj(cx, mods, ctx_row, w_in, cos_c, sin_c, ctx_rope, cs, layer=l, tm=tc,
                   grid_rows_out=False)
        q_x, k_x, v_x, g_x, nq_x, nk_x, nv_x = [a.reshape(b, n, -1) for a in px[:7]]
        gt_x = px[9]
        q_c, k_c, v_c, g_c, nq_c, nk_c, nv_c, zre_c, zim_c, gt_c = [a.reshape(b, n_ctx, -1) for a in pc]

        ret_x = _retention(lg_all, q_x, k_x, v_x, g_x, gn_all, k_c, v_c, layer=l)
        bias = _na_bias_tables(na_rpb[l])
        na_x = _neighbourhood_attention(nq_x, nk_x, nv_x, nk_c, nv_c, bias)
        a_re, a_im = _dft_rows(px[7].reshape(b, rows, -1), px[8].reshape(b, rows, -1), w2)
        fou_x = _dft_cols(a_re, a_im, twc, tws, kmat, wf, layer=l, scale=fou_scale_x)
        xs = _merge(xs, mods, ffn_row, ret_x.reshape(b * n, -1), na_x.reshape(b * n, -1),
                    fou_x.reshape(b * n, d), gt_x, wr, wn, wo, ln,
                    layer=l, tm=tm_ffn, alpha=alpha)
        xs = _ffn(xs, mods, ffn_row, *ffb, ln, layer=l, tm=tm_ffn, mod_base=6, ln_row=2, alpha=alpha)
        if not last:
            ret_c = _retention(lg_all, q_c, k_c, v_c, g_c, gn_all, layer=l)
            na_c = _context_attention(nq_c, nk_c, nv_c)
            fou_c = _ctx_fourier(zre_c, zim_c, cmat_ctx, wf, layer=l, scale=fou_scale_c)
            cx = _merge(cx, mods, ctx_row, ret_c.reshape(b * n_ctx, -1), na_c.reshape(b * n_ctx, -1),
                        fou_c.reshape(b * n_ctx, d), gt_c.reshape(b * n_ctx, -1), wr, wn, wo, ln,
                        layer=l, tm=tc_ffn, alpha=alpha)
            cx = _ffn(cx, mods, ctx_row, *ffb, ln, layer=l, tm=tc_ffn, mod_base=6, ln_row=2,
                      alpha=alpha)
    return xs.reshape(b, n, d)
```

```python
import functools
import math

import numpy as np
import jax
import jax.numpy as jnp
from jax import lax
from jax.experimental import pallas as pl
from jax.experimental.pallas import tpu as pltpu

F32 = jnp.float32
BF16 = jnp.bfloat16

GRID_W = 64
RET_HEADS = 4
RET_DK = 128
RET_DV = 256
NA_HEADS = 8
NA_DH = 64
NA_KH = 8
NA_KW = 16
FOU_GROUPS = 4
FOU_DG = 128
N_BRANCH = 3
ROPE_BASE = 10000.0
LN_EPS = 1e-6

V7X_VMEM_BYTES = 64 * 1024 * 1024
V7X_LANES = 128
V7X_SUBLANES = 8

TOKEN_TILE = 512
RET_CHUNK = 256
FFN_TILE = 1024
TOKEN_SPLIT = 4
SUBTILE_MIN = 256
RET_UNROLL = 8
NA_GROUP_ROWS = 4
NA_KEY_ROWS = NA_GROUP_ROWS + NA_KH
FOU_K1_BLOCK = V7X_SUBLANES
FOU_A_ROWS = 2 * V7X_SUBLANES
NEG_BIG = -1e30
LOG2_E = math.log2(math.e)


def _cparams(sem, vmem_mb):
    return pltpu.CompilerParams(dimension_semantics=sem,
                                vmem_limit_bytes=int(vmem_mb * 1024 * 1024))


def _resident(shape, layer=None):
    nd = len(shape)
    if layer is None:
        return pl.BlockSpec(shape, lambda *_: (0,) * nd, pipeline_mode=pl.Buffered(1))
    return pl.BlockSpec((None,) + tuple(shape), lambda *_: (layer,) + (0,) * nd,
                        pipeline_mode=pl.Buffered(1))


def _row_splits(tm):
    n = max(1, min(TOKEN_SPLIT, tm // SUBTILE_MIN))
    step = tm // n
    return [slice(s * step, (s + 1) * step) for s in range(n)]


def _ln_rows(x):
    mu = jnp.mean(x, axis=-1, keepdims=True)
    xc = x - mu
    var = jnp.mean(xc * xc, axis=-1, keepdims=True)
    return xc * lax.rsqrt(var + LN_EPS)


def _sigmoid(x):
    return 1.0 / (1.0 + jnp.exp(-x))


def _dot(a, b):
    return jnp.dot(a, b, preferred_element_type=F32)


def _dot_nt(a, b):
    return lax.dot_general(a, b, (((1,), (1,)), ((), ())), preferred_element_type=F32)


def _dot_tn(a, b):
    return lax.dot_general(a, b, (((0,), (0,)), ((), ())), preferred_element_type=F32)


def _mods_kernel(c_ref, w_ref, b_ref, o_ref):
    c = c_ref[...]
    h = (c * _sigmoid(c)).astype(BF16)
    o_ref[0] = _dot(h, w_ref[0].astype(BF16)) + b_ref[0]


def _ada_mods(cvec, ada_w, ada_b):
    depth, d, nine_d = ada_w.shape
    wb = 3 * d
    nblk = nine_d // wb
    return pl.pallas_call(
        _mods_kernel,
        grid=(depth, nblk),
        in_specs=[pl.BlockSpec((8, d), lambda l, j: (0, 0)),
                  pl.BlockSpec((1, d, wb), lambda l, j: (l, 0, j)),
                  pl.BlockSpec((1, 1, wb), lambda l, j: (l, 0, j))],
        out_specs=pl.BlockSpec((1, 8, wb), lambda l, j: (l, 0, j)),
        out_shape=jax.ShapeDtypeStruct((depth, 8, nine_d), F32),
        compiler_params=_cparams(("parallel", "parallel"), 40),
        name="ada_mods",
    )(cvec, ada_w, ada_b.reshape(depth, 1, nine_d))


def _ffn_kernel(x_ref, mods_ref, w1_ref, w3_ref, w2_ref, ln_ref, o_ref, *, mod_base, ln_row, alpha):
    m = mods_ref[...]
    shift = m[mod_base:mod_base + 1]
    scale = m[mod_base + 1:mod_base + 2]
    gate = m[mod_base + 2:mod_base + 3]
    ln = ln_ref[...]
    for rows in _row_splits(x_ref.shape[0]):
        x = x_ref[rows, :]
        h = (_ln_rows(x) * (1.0 + scale) + shift).astype(BF16)
        a = _dot(h, w1_ref[...])
        b = _dot(h, w3_ref[...])
        g = (a * _sigmoid(a) * b).astype(BF16)
        y = _dot(g, w2_ref[...])
        z = alpha * x + 0.5 * gate * y
        o_ref[rows, :] = _ln_rows(z) * ln[ln_row:ln_row + 1] + ln[3 + ln_row:4 + ln_row]


def _ffn(x2, mods, mods_row, w1, w3, w2, ln, *, layer, tm, mod_base, ln_row, alpha):
    t, d = x2.shape
    dff = w1.shape[2]
    kern = functools.partial(_ffn_kernel, mod_base=mod_base, ln_row=ln_row, alpha=alpha)
    return pl.pallas_call(
        kern,
        grid=(t // tm,),
        in_specs=[pl.BlockSpec((tm, d), lambda i: (i, 0)),
                  pl.BlockSpec((None, None, 9, d), lambda i: (layer, mods_row(i), 0, 0)),
                  _resident((d, dff), layer), _resident((d, dff), layer), _resident((dff, d), layer),
                  _resident((6, d), layer)],
        out_specs=pl.BlockSpec((tm, d), lambda i: (i, 0)),
        out_shape=jax.ShapeDtypeStruct((t, d), F32),
        compiler_params=_cparams(("parallel",), 62 if tm > TOKEN_TILE else 56),
        name="ffn",
    )(x2, mods, w1, w3, w2, ln)


def _rope(t, cos, sin_signed, first_half):
    swapped = jnp.where(first_half, pltpu.roll(t, 96, 1), pltpu.roll(t, 32, 1))
    return t * cos + swapped * sin_signed


def _proj_kernel(x_ref, mods_ref, w_ref, cos_ref, sin_ref, cs_ref, *rest,
                 q_scale, nq_scale, grid_rows_out):
    extra = rest[:-10]
    q_ref, k_ref, v_ref, g_ref, nq_ref, nk_ref, nv_ref, zre_ref, zim_ref, gt_ref = rest[-10:]
    x = x_ref[...]
    m = mods_ref[...]
    h = (_ln_rows(x) * (1.0 + m[4:5]) + m[3:4]).astype(BF16)
    tm = x.shape[0]
    cw = 512

    def seg(j):
        return _dot(h, w_ref[:, j * cw:(j + 1) * cw])

    cos = cos_ref[...]
    sin = sin_ref[...]
    lane = lax.broadcasted_iota(jnp.int32, (tm, V7X_LANES), 1)
    first_half = (lane & 32) == 0

    cs = cs_ref[...]
    u = seg(9).astype(BF16)
    r = seg(0) * q_scale
    for hh in range(RET_HEADS):
        sl = slice(hh * RET_DK, (hh + 1) * RET_DK)
        q_ref[:, sl] = _rope(r[:, sl], cos, sin, first_half).astype(BF16)
    if grid_rows_out:
        perm_ref, = extra
        u = _dot(perm_ref[...], u).astype(BF16)
    r = seg(1)
    for hh in range(RET_HEADS):
        sl = slice(hh * RET_DK, (hh + 1) * RET_DK)
        k_ref[:, sl] = _rope(r[:, sl], cos, sin, first_half).astype(BF16)
    n_r = tm // GRID_W

    def fourier(gi):
        sl = slice(gi * FOU_DG, (gi + 1) * FOU_DG)
        z = _dot(u[:, sl], cs)
        if grid_rows_out:
            for c in range(GRID_W):
                dst = slice((c * FOU_GROUPS + gi) * FOU_DG, (c * FOU_GROUPS + gi + 1) * FOU_DG)
                zre_ref[:, dst] = z[c * n_r:(c + 1) * n_r, :FOU_DG]
                zim_ref[:, dst] = z[c * n_r:(c + 1) * n_r, FOU_DG:]
        else:
            zre_ref[:, sl] = z[:, :FOU_DG].astype(BF16)
            zim_ref[:, sl] = z[:, FOU_DG:].astype(BF16)

    for j in range(2):
        r = seg(4 + j)
        g_ref[:, j * cw:(j + 1) * cw] = (r * _sigmoid(r)).astype(BF16)
        fourier(j)
    for j in range(6):
        gt_ref[:, j * cw:(j + 1) * cw] = _sigmoid(seg(10 + j)).astype(BF16)
        if j < FOU_GROUPS - 2:
            fourier(2 + j)
    nq_ref[...] = (seg(6) * nq_scale).astype(BF16)
    for j in range(2):
        v_ref[:, j * cw:(j + 1) * cw] = seg(2 + j).astype(BF16)
    nk_ref[...] = seg(7).astype(BF16)
    nv_ref[...] = seg(8).astype(BF16)


def _proj(x2, mods, mods_row, w_in, cos, sin, rope_row, cs, *, layer, tm, grid_rows_out):
    t, d = x2.shape
    d_in = w_in.shape[2]
    ch = FOU_GROUPS * FOU_DG
    widths = (512, 512, 1024, 1024, 512, 512, 512, ch, ch, 3072)
    kern = functools.partial(_proj_kernel, q_scale=RET_DK ** -0.5, nq_scale=NA_DH ** -0.5 * LOG2_E,
                             grid_rows_out=grid_rows_out)
    out_specs = [pl.BlockSpec((tm, w), lambda i: (i, 0)) for w in widths]
    out_shape = [jax.ShapeDtypeStruct((t, w), BF16) for w in widths]
    in_specs = [pl.BlockSpec((tm, d), lambda i: (i, 0)),
                pl.BlockSpec((None, None, 9, d), lambda i: (layer, mods_row(i), 0, 0)),
                _resident((d, d_in), layer),
                pl.BlockSpec((tm, V7X_LANES), lambda i: (rope_row(i), 0)),
                pl.BlockSpec((tm, V7X_LANES), lambda i: (rope_row(i), 0)),
                _resident((FOU_DG, 2 * FOU_DG))]
    args = [x2, mods, w_in, cos, sin, cs]
    if grid_rows_out:
        n_r = tm // GRID_W
        for o in (7, 8):
            out_specs[o] = pl.BlockSpec((n_r, GRID_W * ch), lambda i: (i, 0))
            out_shape[o] = jax.ShapeDtypeStruct((t // GRID_W, GRID_W * ch), F32)
        src = (np.arange(n_r)[None, :] * GRID_W + np.arange(GRID_W)[:, None]).reshape(-1)
        perm = np.zeros((tm, tm), np.float32)
        perm[np.arange(tm), src] = 1.0
        in_specs.append(_resident((tm, tm)))
        args.append(jnp.asarray(perm).astype(BF16))
    return pl.pallas_call(
        kern,
        grid=(t // tm,),
        in_specs=in_specs,
        out_specs=out_specs,
        out_shape=out_shape,
        compiler_params=_cparams(("parallel",), 56),
        name="mixer_in_proj",
    )(*args)


def _log_sigmoid(x):
    return jnp.minimum(x, 0.0) - jnp.log(1.0 + jnp.exp(-jnp.abs(x)))


def _kv_update(k, v, lf, lb, idx):
    n = k.shape[0]
    kf = k.astype(F32)
    kk = jnp.concatenate([(kf * jnp.exp(lf * (n - 1.0 - idx))).astype(BF16),
                          (kf * jnp.exp(lb * idx)).astype(BF16)], axis=1)
    return _dot_tn(kk, v)


def _ret_kernel(*refs, n_chunks, chunk, with_prefix):
    if with_prefix:
        lg_ref, q_ref, k_ref, v_ref, g_ref, gn_ref, kc_ref, vc_ref, o_ref, u_ref, s_ref = refs
    else:
        lg_ref, q_ref, k_ref, v_ref, g_ref, gn_ref, o_ref, u_ref, s_ref = refs
    c = chunk
    dk, dv = RET_DK, RET_DV
    ls = _log_sigmoid(lg_ref[...])
    lf = ls[0, 0:1, :]
    lb = ls[1, 0:1, :]
    lf_v = jnp.concatenate([lf] * (dv // V7X_LANES), axis=1)
    lb_v = jnp.concatenate([lb] * (dv // V7X_LANES), axis=1)
    lf_c = jnp.concatenate([lf] * (c // V7X_LANES), axis=1)
    lb_c = jnp.concatenate([lb] * (c // V7X_LANES), axis=1)
    idx = lax.broadcasted_iota(jnp.int32, (c, V7X_LANES), 0).astype(F32)

    def upd(i, carry):
        r0 = pl.multiple_of(i * c, c)
        u_ref[i] = _kv_update(k_ref[pl.ds(r0, c), :], v_ref[pl.ds(r0, c), :], lf, lb, idx)
        return carry
    lax.fori_loop(0, n_chunks, upd, 0, unroll=min(RET_UNROLL, n_chunks))

    if with_prefix:
        n_ctx = kc_ref.shape[0]
        idx_c = lax.broadcasted_iota(jnp.int32, (n_ctx, V7X_LANES), 0).astype(F32)
        u0 = _kv_update(kc_ref[...], vc_ref[...], lf, lb, idx_c)
        sf0, sb0 = u0[:dk], u0[dk:]
    else:
        sf0 = jnp.zeros((dk, dv), F32)
        sb0 = sf0
    gf_c = jnp.exp(lf_v * float(c))
    gb_c = jnp.exp(lb_v * float(c))

    def scan_f(i, sf):
        s_ref[i, :dk, :] = sf.astype(BF16)
        return sf * gf_c + u_ref[i, :dk, :]
    lax.fori_loop(0, n_chunks, scan_f, sf0)

    def scan_b(t, sb):
        i = n_chunks - 1 - t
        s_ref[i, dk:, :] = sb.astype(BF16)
        return sb * gb_c + u_ref[i, dk:, :]
    lax.fori_loop(0, n_chunks, scan_b, sb0)

    pr = (lax.broadcasted_iota(jnp.int32, (c, c), 0)
          - lax.broadcasted_iota(jnp.int32, (c, c), 1)).astype(F32)
    dmat = jnp.where(pr >= 0.0, jnp.exp(lf_c * jnp.maximum(pr, 0.0)),
                     jnp.exp(lb_c * jnp.maximum(-pr, 0.0)))
    qdf = jnp.exp(lf * (idx + 1.0))
    qdb = jnp.exp(lb * (float(c) - idx))
    gn = gn_ref[...]
    gn_w, gn_b = gn[0:1], gn[1:2]

    def out(i, carry):
        r0 = pl.multiple_of(i * c, c)
        q = q_ref[pl.ds(r0, c), :]
        k = k_ref[pl.ds(r0, c), :]
        v = v_ref[pl.ds(r0, c), :]
        inner = (_dot_nt(q, k) * dmat).astype(BF16)
        qf = q.astype(F32)
        qq = jnp.concatenate([(qf * qdf).astype(BF16), (qf * qdb).astype(BF16)], axis=1)
        o = _dot(inner, v) + _dot(qq, s_ref[i])
        o = _ln_rows(o) * gn_w + gn_b
        o_ref[pl.ds(r0, c), :] = (o * g_ref[pl.ds(r0, c), :].astype(F32)).astype(BF16)
        return carry
    lax.fori_loop(0, n_chunks, out, 0, unroll=min(RET_UNROLL, n_chunks))


def _retention(lg, q, k, v, g, gn, kc=None, vc=None, *, layer):
    b, n, _ = q.shape
    c = min(RET_CHUNK, n)
    nc = n // c
    with_prefix = kc is not None
    kern = functools.partial(_ret_kernel, n_chunks=nc, chunk=c, with_prefix=with_prefix)
    in_specs = [pl.BlockSpec((None, None, 2, 8, V7X_LANES), lambda bi, hi: (layer, hi, 0, 0, 0)),
                pl.BlockSpec((None, n, RET_DK), lambda bi, hi: (bi, 0, hi)),
                pl.BlockSpec((None, n, RET_DK), lambda bi, hi: (bi, 0, hi)),
                pl.BlockSpec((None, n, RET_DV), lambda bi, hi: (bi, 0, hi)),
                pl.BlockSpec((None, n, RET_DV), lambda bi, hi: (bi, 0, hi)),
                pl.BlockSpec((None, None, 2, RET_DV), lambda bi, hi: (layer, hi, 0, 0))]
    args = [lg, q, k, v, g, gn]
    if with_prefix:
        n_ctx = kc.shape[1]
        in_specs += [pl.BlockSpec((None, n_ctx, RET_DK), lambda bi, hi: (bi, 0, hi)),
                     pl.BlockSpec((None, n_ctx, RET_DV), lambda bi, hi: (bi, 0, hi))]
        args += [kc, vc]
    return pl.pallas_call(
        kern,
        grid=(b, RET_HEADS),
        in_specs=in_specs,
        out_specs=pl.BlockSpec((None, n, RET_DV), lambda bi, hi: (bi, 0, hi)),
        out_shape=jax.ShapeDtypeStruct((b, n, RET_HEADS * RET_DV), BF16),
        scratch_shapes=[pltpu.VMEM((nc, 2 * RET_DK, RET_DV), F32),
                        pltpu.VMEM((nc, 2 * RET_DK, RET_DV), BF16)],
        compiler_params=_cparams(("parallel", "parallel"), 56),
        name="retention",
    )(*args)


def _softmax_pv(s_parts, v_parts):
    m = None
    for s in s_parts:
        mi = jnp.max(s, axis=-1, keepdims=True)
        m = mi if m is None else jnp.maximum(m, mi)
    l = None
    o = None
    for s, v in zip(s_parts, v_parts):
        p = jnp.exp2(s - m)
        li = jnp.sum(p, axis=-1, keepdims=True)
        oi = _dot(p.astype(BF16), v)
        l = li if l is None else l + li
        o = oi if o is None else o + oi
    return o / l


def _na_kernel(q_ref, k_ref, v_ref, kc_ref, vc_ref, bias_ref, o_ref, s_ref, *, n_groups, key_row0_max):
    gq = NA_GROUP_ROWS * GRID_W
    nk = NA_KEY_ROWS * GRID_W
    lane = lax.broadcasted_iota(jnp.int32, (gq, V7X_LANES), 1)
    head0 = lane < NA_DH
    kc = kc_ref[...]
    vc = vc_ref[...]

    def key_start(g):
        kb = jnp.clip(g * NA_GROUP_ROWS - NA_KH // 2, 0, key_row0_max)
        return pl.multiple_of(kb * GRID_W, GRID_W)

    def scores(g, slot):
        q0 = pl.multiple_of(g * gq, gq)
        cfg = jnp.where(g == 0, 0, jnp.where(g == n_groups - 1, 2, 1))
        q = q_ref[pl.ds(q0, gq), :]
        kl = k_ref[pl.ds(key_start(g), nk), :]
        zero = jnp.zeros_like(q)
        for hh in range(2):
            qh = jnp.where(head0, q, zero) if hh == 0 else jnp.where(head0, zero, q)
            s_ref[slot, hh, :, :nk] = _dot_nt(qh, kl) + bias_ref[cfg, hh]
            s_ref[slot, hh, :, nk:] = _dot_nt(qh, kc)

    def finish(g, slot):
        q0 = pl.multiple_of(g * gq, gq)
        vl = v_ref[pl.ds(key_start(g), nk), :]
        outs = [_softmax_pv([s_ref[slot, hh, :, :nk], s_ref[slot, hh, :, nk:]], [vl, vc])
                for hh in range(2)]
        o_ref[pl.ds(q0, gq), :] = jnp.where(head0, outs[0], outs[1]).astype(BF16)

    scores(0, 0)

    def pair(j, carry):
        g = 2 * j
        scores(g + 1, 1)
        finish(g, 0)
        scores(jnp.minimum(g + 2, n_groups - 1), 0)
        finish(g + 1, 1)
        return carry
    lax.fori_loop(0, n_groups // 2, pair, 0, unroll=2)


def _na_bias_tables(rpb):
    h = rpb.shape[0]
    w = GRID_W
    cols = np.arange(w)
    cs = np.clip(cols - NA_KW // 2, 0, w - NA_KW)
    kcol = np.arange(w)[None, :]
    colmask = (kcol >= cs[:, None]) & (kcol < cs[:, None] + NA_KW)
    pad = w - 1
    off = pad + NA_KW - 1
    period = off + w + 1
    rpb_p = jnp.pad(rpb, ((0, 0), (0, 0), (pad, period - pad - rpb.shape[2])))
    skew = jnp.tile(rpb_p, (1, 1, w))[:, :, :w * (period - 1)].reshape(h, -1, w, period - 1)
    toep = skew[:, :, :, off:off + w]
    toep = jnp.where(colmask[None, None], toep * LOG2_E, NEG_BIG)
    n_dr = toep.shape[1]
    toep = jnp.concatenate([toep, jnp.full((h, 1, w, w), NEG_BIG, F32)], axis=1)
    dr = np.full((3, NA_GROUP_ROWS, NA_KEY_ROWS), n_dr, np.int32)
    for cfg in range(3):
        rel = NA_GROUP_ROWS * cfg
        for rr in range(NA_GROUP_ROWS):
            lo = (0, rr, NA_KEY_ROWS - NA_KH)[cfg]
            for i in range(lo, lo + NA_KH):
                dr[cfg, rr, i] = i - rel - rr + NA_KH - 1
    pairs = dr.reshape(3, NA_GROUP_ROWS, NA_KEY_ROWS // 2, 2)
    uniq = sorted({(int(a), int(c)) for a, c in pairs.reshape(-1, 2)})
    slot_of = {p: u for u, p in enumerate(uniq)}
    piece = np.array([[[slot_of[(int(a), int(c))] for a, c in row] for row in cfg] for cfg in pairs])
    left = jnp.take(toep, jnp.asarray([p[0] for p in uniq]), axis=1)
    right = jnp.take(toep, jnp.asarray([p[1] for p in uniq]), axis=1)
    pieces = jnp.concatenate([left, right], axis=-1)
    n_u = len(uniq)

    def assemble(p_ref, o_ref):
        for cfg in range(3):
            for rr in range(NA_GROUP_ROWS):
                for pc in range(NA_KEY_ROWS // 2):
                    o_ref[cfg, rr * w:(rr + 1) * w, pc * 2 * w:(pc + 1) * 2 * w] = (
                        p_ref[int(piece[cfg, rr, pc])])

    return pl.pallas_call(
        assemble,
        grid=(h,),
        in_specs=[pl.BlockSpec((None, n_u, w, 2 * w), lambda hi: (hi, 0, 0, 0))],
        out_specs=pl.BlockSpec((3, None, NA_GROUP_ROWS * w, NA_KEY_ROWS * w),
                               lambda hi: (0, hi, 0, 0)),
        out_shape=jax.ShapeDtypeStruct((3, h, NA_GROUP_ROWS * w, NA_KEY_ROWS * w), F32),
        compiler_params=_cparams(("parallel",), 32),
        name="na_bias_table",
    )(pieces)


def _neighbourhood_attention(nq, nk, nv, kc, vc, bias):
    b, n, hw = nq.shape
    rows = n // GRID_W
    n_ctx = kc.shape[1]
    n_groups = rows // NA_GROUP_ROWS
    pairs = NA_HEADS // 2
    gq = NA_GROUP_ROWS * GRID_W
    nkeys = NA_KEY_ROWS * GRID_W
    kern = functools.partial(_na_kernel, n_groups=n_groups, key_row0_max=rows - NA_KEY_ROWS)
    tok = pl.BlockSpec((None, n, V7X_LANES), lambda bi, pi: (bi, 0, pi))
    ctx = pl.BlockSpec((None, n_ctx, V7X_LANES), lambda bi, pi: (bi, 0, pi))
    return pl.pallas_call(
        kern,
        grid=(b, pairs),
        in_specs=[tok, tok, tok, ctx, ctx,
                  pl.BlockSpec((3, 2, gq, nkeys), lambda bi, pi: (0, pi, 0, 0))],
        out_specs=tok,
        out_shape=jax.ShapeDtypeStruct((b, n, hw), BF16),
        scratch_shapes=[pltpu.VMEM((2, 2, gq, nkeys + n_ctx), F32)],
        compiler_params=_cparams(("parallel", "parallel"), 48),
        name="neighbourhood_attention",
    )(nq, nk, nv, kc, vc, bias)


def _ctx_attn_kernel(q_ref, k_ref, v_ref, o_ref):
    q = q_ref[...]
    k = k_ref[...]
    v = v_ref[...]
    lane = lax.broadcasted_iota(jnp.int32, q.shape, 1)
    head0 = lane < NA_DH
    zero = jnp.zeros_like(q)
    outs = []
    for hh in range(2):
        qh = jnp.where(head0, q, zero) if hh == 0 else jnp.where(head0, zero, q)
        outs.append(_softmax_pv([_dot_nt(qh, k)], [v]))
    o_ref[...] = jnp.where(head0, outs[0], outs[1]).astype(BF16)


def _context_attention(q, k, v):
    b, n_ctx, hw = q.shape
    blk = pl.BlockSpec((None, n_ctx, V7X_LANES), lambda bi, pi: (bi, 0, pi))
    return pl.pallas_call(
        _ctx_attn_kernel,
        grid=(b, NA_HEADS // 2),
        in_specs=[blk, blk, blk],
        out_specs=blk,
        out_shape=jax.ShapeDtypeStruct((b, n_ctx, hw), BF16),
        compiler_params=_cparams(("parallel", "parallel"), 32),
        name="context_attention",
    )(q, k, v)


def _dft_rows_kernel(zre_ref, zim_ref, w_ref, are_ref, aim_ref):
    r = zre_ref.shape[0]
    z = jnp.concatenate([zre_ref[...].astype(BF16), zim_ref[...].astype(BF16)], axis=0)
    a = _dot(w_ref[...], z)
    are_ref[...] = a[:r].astype(BF16)
    aim_ref[...] = a[r:].astype(BF16)


def _dft_rows(zre, zim, w2):
    b, r, wc = zre.shape
    cb = min(4096, wc)
    blk = pl.BlockSpec((None, r, cb), lambda bi, ci: (bi, 0, ci))
    return pl.pallas_call(
        _dft_rows_kernel,
        grid=(b, wc // cb),
        in_specs=[blk, blk, _resident((2 * r, 2 * r))],
        out_specs=[blk, blk],
        out_shape=[jax.ShapeDtypeStruct((b, r, wc), BF16)] * 2,
        compiler_params=_cparams(("parallel", "parallel"), 40),
        name="dft_rows",
    )(zre, zim, w2)


def _dft_cols_kernel(are_ref, aim_ref, twc_ref, tws_ref, km_ref, wo_ref, o_ref, *, scale, ch, jb):
    w = are_ref.shape[1] // ch
    reps = ch // V7X_LANES
    slabs_re = [are_ref[:, c * ch:(c + 1) * ch].astype(F32) for c in range(w)]
    slabs_im = [aim_ref[:, c * ch:(c + 1) * ch].astype(F32) for c in range(w)]
    for hf in range(are_ref.shape[0] // jb):
        rs = slice(hf * jb, (hf + 1) * jb)
        ts_rows = slice(hf * jb * w, (hf + 1) * jb * w)
        ar = jnp.concatenate([s[rs] for s in slabs_re], axis=0)
        ai = jnp.concatenate([s[rs] for s in slabs_im], axis=0)
        tc = jnp.concatenate([twc_ref[ts_rows, :]] * reps, axis=1)
        ts = jnp.concatenate([tws_ref[ts_rows, :]] * reps, axis=1)
        a = jnp.concatenate([(ar * tc - ai * ts).astype(BF16), (ar * ts + ai * tc).astype(BF16)],
                            axis=0)
        y = _dot(km_ref[...], a) * scale
        out = _dot(y.astype(BF16), wo_ref[...])
        o_ref[:, rs, :] = out.reshape(w, jb, out.shape[1])


def _dft_cols(a_re, a_im, twc, tws, kmat, w_out, *, layer, scale):
    b, r, wc = a_re.shape
    w = GRID_W
    ch = wc // w
    d = w_out.shape[2]
    jb = FOU_K1_BLOCK
    ab = FOU_A_ROWS
    blk = pl.BlockSpec((None, ab, wc), lambda bi, ji: (bi, ji, 0))
    tw = pl.BlockSpec((ab * w, V7X_LANES), lambda bi, ji: (ji, 0))
    kern = functools.partial(_dft_cols_kernel, scale=scale, ch=ch, jb=jb)
    return pl.pallas_call(
        kern,
        grid=(b, r // ab),
        in_specs=[blk, blk, tw, tw, _resident(kmat.shape), _resident((ch, d), layer)],
        out_specs=pl.BlockSpec((None, w, ab, d), lambda bi, ji: (bi, 0, ji, 0)),
        out_shape=jax.ShapeDtypeStruct((b, w, r, d), F32),
        compiler_params=_cparams(("parallel", "parallel"), 40),
        name="dft_cols",
    )(a_re, a_im, twc, tws, kmat, w_out)


def _ctx_fourier_kernel(zre_ref, zim_ref, cm_ref, wo_ref, o_ref, *, scale):
    z = jnp.concatenate([zre_ref[...], zim_ref[...]], axis=0)
    y = _dot(cm_ref[...], z) * scale
    o_ref[...] = _dot(y.astype(BF16), wo_ref[...])


def _ctx_fourier(zre, zim, cmat, w_out, *, layer, scale):
    b, n_ctx, ch = zre.shape
    d = w_out.shape[2]
    blk = pl.BlockSpec((None, n_ctx, ch), lambda bi: (bi, 0, 0))
    kern = functools.partial(_ctx_fourier_kernel, scale=scale)
    return pl.pallas_call(
        kern,
        grid=(b,),
        in_specs=[blk, blk, _resident(cmat.shape), _resident((ch, d), layer)],
        out_specs=pl.BlockSpec((None, n_ctx, d), lambda bi: (bi, 0, 0)),
        out_shape=jax.ShapeDtypeStruct((b, n_ctx, d), F32),
        compiler_params=_cparams(("parallel",), 32),
        name="ctx_fourier",
    )(zre, zim, cmat, w_out)


def _merge_kernel(x_ref, mods_ref, ret_ref, na_ref, fou_ref, gt_ref, wr_ref, wn_ref, wo_ref, ln_ref,
                  o_ref, *, alpha):
    d = x_ref.shape[1]
    m = mods_ref[...]
    ln = ln_ref[...]
    for rows in _row_splits(x_ref.shape[0]):
        y_ret = _dot(ret_ref[rows, :], wr_ref[...])
        y_na = _dot(na_ref[rows, :], wn_ref[...])
        mix = (gt_ref[rows, 0:d].astype(F32) * y_ret + gt_ref[rows, d:2 * d].astype(F32) * y_na
               + gt_ref[rows, 2 * d:3 * d].astype(F32) * fou_ref[rows, :])
        y = _dot(mix.astype(BF16), wo_ref[...])
        z = alpha * x_ref[rows, :] + m[5:6] * y
        o_ref[rows, :] = _ln_rows(z) * ln[1:2] + ln[4:5]


def _merge(x2, mods, mods_row, ret, na, fou, gt, wr, wn, wo, ln, *, layer, tm, alpha):
    t, d = x2.shape
    kern = functools.partial(_merge_kernel, alpha=alpha)

    def rows(wd):
        return pl.BlockSpec((tm, wd), lambda i: (i, 0))
    return pl.pallas_call(
        kern,
        grid=(t // tm,),
        in_specs=[rows(d), pl.BlockSpec((None, None, 9, d), lambda i: (layer, mods_row(i), 0, 0)),
                  rows(ret.shape[1]), rows(na.shape[1]), rows(d), rows(gt.shape[1]),
                  _resident(wr.shape[1:], layer), _resident(wn.shape[1:], layer),
                  _resident(wo.shape[1:], layer), _resident((6, d), layer)],
        out_specs=rows(d),
        out_shape=jax.ShapeDtypeStruct((t, d), F32),
        compiler_params=_cparams(("parallel",), 60 if tm > TOKEN_TILE else 48),
        name="merge",
    )(x2, mods, ret, na, fou, gt, wr, wn, wo, ln)


def _rope_tables(n):
    half = RET_DK // 4
    t = np.arange(n)
    row, col = t // GRID_W, t % GRID_W
    freqs = ROPE_BASE ** (-np.arange(half, dtype=np.float64) / half)
    ang_r = row.astype(np.float64)[:, None] * freqs[None, :]
    ang_c = col.astype(np.float64)[:, None] * freqs[None, :]
    cos = np.concatenate([np.cos(ang_r)] * 2 + [np.cos(ang_c)] * 2, axis=1)
    sin = np.concatenate([-np.sin(ang_r), np.sin(ang_r), -np.sin(ang_c), np.sin(ang_c)], axis=1)
    return jnp.asarray(cos, F32), jnp.asarray(sin, F32)


def _dft_tables(n):
    w = GRID_W
    r = n // w
    jb = FOU_K1_BLOCK
    two_pi = 2.0 * np.pi
    ch = np.arange(FOU_DG)
    ang = two_pi * ((ch[:, None] * ch[None, :]) % FOU_DG) / FOU_DG
    cs = np.concatenate([np.cos(ang), np.sin(ang)], axis=1)
    k1 = np.arange(r)
    ang = two_pi * ((k1[:, None] * k1[None, :]) % r) / r
    w2 = np.block([[np.cos(ang), -np.sin(ang)], [np.sin(ang), np.cos(ang)]])
    c = np.arange(w)
    k1_blocks = k1.reshape(r // jb, 1, jb)
    ang = two_pi * (k1_blocks * c[None, :, None]).reshape(-1) / n
    twc = np.repeat(np.cos(ang)[:, None], V7X_LANES, axis=1)
    tws = np.repeat(np.sin(ang)[:, None], V7X_LANES, axis=1)
    ang = two_pi * ((c[:, None] * c[None, :]) % w) / w
    eye = np.eye(jb)
    kc = np.einsum("kc,ji->kjci", np.cos(ang), eye).reshape(w * jb, w * jb)
    ks = np.einsum("kc,ji->kjci", np.sin(ang), eye).reshape(w * jb, w * jb)
    kmat = np.concatenate([kc, -ks], axis=1)
    as_bf16 = lambda a: jnp.asarray(a, F32).astype(BF16)
    return (as_bf16(cs), as_bf16(w2), jnp.asarray(twc, F32), jnp.asarray(tws, F32), as_bf16(kmat))


def _ctx_dft_matrix(n_ctx):
    m = np.arange(n_ctx)
    ang = 2.0 * np.pi * ((m[:, None] * m[None, :]) % n_ctx) / n_ctx
    return jnp.asarray(np.concatenate([np.cos(ang), -np.sin(ang)], axis=1), F32).astype(BF16)


def kernel(x, c, ctx, c_ctx, ada_w, ada_b, ln_w, ln_b, ffa_w1, ffa_w3, ffa_w2, mix_w_in,
           ret_decay_logit, ret_gn_w, ret_gn_b, ret_w_out, na_rpb, na_w_out, fou_w_out, mix_w_o,
           ffb_w1, ffb_w3, ffb_w2):
    b, n, d = x.shape
    n_ctx = ctx.shape[1]
    depth = ada_w.shape[0]
    rows = n // GRID_W
    assert n % TOKEN_TILE == 0 and rows % (2 * NA_GROUP_ROWS) == 0 and rows >= NA_KEY_ROWS
    assert rows % FOU_A_ROWS == 0 and b + 1 <= 8
    alpha = (2 * depth) ** 0.25
    tm = TOKEN_TILE
    tm_ffn = FFN_TILE if n % FFN_TILE == 0 else TOKEN_TILE
    tiles_per_seq = n // tm
    lat_row = lambda i: i // tiles_per_seq
    ffn_row = lambda i: i // (n // tm_ffn)
    ctx_row = lambda i: b
    lat_rope = lambda i: i % tiles_per_seq
    ctx_rope = lambda i: 0
    t_ctx = b * n_ctx
    tc = TOKEN_TILE if t_ctx % TOKEN_TILE == 0 else n_ctx
    tc_ffn = FFN_TILE if t_ctx % FFN_TILE == 0 else tc

    cos_x, sin_x = _rope_tables(n)
    cos_c = jnp.ones((tc, V7X_LANES), F32)
    sin_c = jnp.zeros((tc, V7X_LANES), F32)
    cs, w2, twc, tws, kmat = _dft_tables(n)
    cmat_ctx = _ctx_dft_matrix(n_ctx)
    fou_scale_x = 1.0 / math.sqrt(n * FOU_DG)
    fou_scale_c = 1.0 / math.sqrt(n_ctx * FOU_DG)

    cvec = jnp.concatenate([c, c_ctx[None, :], jnp.zeros((8 - b - 1, d), F32)], axis=0)
    mods_all = _ada_mods(cvec, ada_w, ada_b).reshape(depth, 8, 9, d)
    ln_all = jnp.concatenate([ln_w, ln_b], axis=1)
    to_bf16 = lambda a: a.astype(BF16)
    ffa = [to_bf16(a) for a in (ffa_w1, ffa_w3, ffa_w2)]
    ffb = [to_bf16(a) for a in (ffb_w1, ffb_w3, ffb_w2)]
    w_in, wr, wn, wf, wo = [to_bf16(a) for a in (mix_w_in, ret_w_out, na_w_out, fou_w_out, mix_w_o)]
    lg_all = jnp.broadcast_to(
        jnp.transpose(ret_decay_logit, (0, 2, 1))[:, :, :, None, None],
        (depth, RET_HEADS, 2, 8, V7X_LANES)).astype(F32)
    gn_all = jnp.stack([ret_gn_w.reshape(depth, RET_HEADS, RET_DV),
                        ret_gn_b.reshape(depth, RET_HEADS, RET_DV)], axis=2)

    xs = x.reshape(b * n, d)
    cx = ctx.reshape(b * n_ctx, d)
    mods, ln = mods_all, ln_all
    for l in range(depth):
        last = l == depth - 1
        xs = _ffn(xs, mods, ffn_row, *ffa, ln, layer=l, tm=tm_ffn, mod_base=0, ln_row=0, alpha=alpha)
        cx = _ffn(cx, mods, ctx_row, *ffa, ln, layer=l, tm=tc_ffn, mod_base=0, ln_row=0, alpha=alpha)

        px = _proj(xs, mods, lat_row, w_in, cos_x, sin_x, lat_rope, cs, layer=l, tm=tm,
                   grid_rows_out=True)
        pc = _proj(cx, mods, ctx_row, w_in, cos_c, sin_c, ctx_rope, cs, layer=l, tm=tc,
                   grid_rows_out=False)
        q_x, k_x, v_x, g_x, nq_x, nk_x, nv_x = [a.reshape(b, n, -1) for a in px[:7]]
        gt_x = px[9]
        q_c, k_c, v_c, g_c, nq_c, nk_c, nv_c, zre_c, zim_c, gt_c = [a.reshape(b, n_ctx, -1) for a in pc]

        ret_x = _retention(lg_all, q_x, k_x, v_x, g_x, gn_all, k_c, v_c, layer=l)
        bias = _na_bias_tables(na_rpb[l])
        na_x = _neighbourhood_attention(nq_x, nk_x, nv_x, nk_c, nv_c, bias)
        a_re, a_im = _dft_rows(px[7].reshape(b, rows, -1), px[8].reshape(b, rows, -1), w2)
        fou_x = _dft_cols(a_re, a_im, twc, tws, kmat, wf, layer=l, scale=fou_scale_x)
        xs = _merge(xs, mods, ffn_row, ret_x.reshape(b * n, -1), na_x.reshape(b * n, -1),
                    fou_x.reshape(b * n, d), gt_x, wr, wn, wo, ln,
                    layer=l, tm=tm_ffn, alpha=alpha)
        xs = _ffn(xs, mods, ffn_row, *ffb, ln, layer=l, tm=tm_ffn, mod_base=6, ln_row=2, alpha=alpha)
        if not last:
            ret_c = _retention(lg_all, q_c, k_c, v_c, g_c, gn_all, layer=l)
            na_c = _context_attention(nq_c, nk_c, nv_c)
            fou_c = _ctx_fourier(zre_c, zim_c, cmat_ctx, wf, layer=l, scale=fou_scale_c)
            cx = _merge(cx, mods, ctx_row, ret_c.reshape(b * n_ctx, -1), na_c.reshape(b * n_ctx, -1),
                        fou_c.reshape(b * n_ctx, d), gt_c.reshape(b * n_ctx, -1), wr, wn, wo, ln,
                        layer=l, tm=tc_ffn, alpha=alpha)
            cx = _ffn(cx, mods, ctx_row, *ffb, ln, layer=l, tm=tc_ffn, mod_base=6, ln_row=2,
                      alpha=alpha)
    return xs.reshape(b, n, d)
```

```python
import functools
import math

import numpy as np
import jax
import jax.numpy as jnp
from jax import lax
from jax.experimental import pallas as pl
from jax.experimental.pallas import tpu as pltpu

F32 = jnp.float32
BF16 = jnp.bfloat16

GRID_W = 64
RET_HEADS = 4
RET_DK = 128
RET_DV = 256
NA_HEADS = 8
NA_DH = 64
NA_KH = 8
NA_KW = 16
FOU_GROUPS = 4
FOU_DG = 128
N_BRANCH = 3
ROPE_BASE = 10000.0
LN_EPS = 1e-6

V7X_VMEM_BYTES = 64 * 1024 * 1024
V7X_LANES = 128
V7X_SUBLANES = 8

TOKEN_TILE = 512
RET_CHUNK = 256
FFN_TILE = 1024
TOKEN_SPLIT = 4
SUBTILE_MIN = 256
RET_UNROLL = 8
NA_GROUP_ROWS = 4
NA_KEY_ROWS = NA_GROUP_ROWS + NA_KH
FOU_K1_BLOCK = V7X_SUBLANES
FOU_A_ROWS = 2 * V7X_SUBLANES
NEG_BIG = -1e30
LOG2_E = math.log2(math.e)


def _cparams(sem, vmem_mb):
    return pltpu.CompilerParams(dimension_semantics=sem,
                                vmem_limit_bytes=int(vmem_mb * 1024 * 1024))


def _resident(shape, layer=None):
    nd = len(shape)
    if layer is None:
        return pl.BlockSpec(shape, lambda *_: (0,) * nd, pipeline_mode=pl.Buffered(1))
    return pl.BlockSpec((None,) + tuple(shape), lambda *_: (layer,) + (0,) * nd,
                        pipeline_mode=pl.Buffered(1))


def _row_splits(tm):
    n = max(1, min(TOKEN_SPLIT, tm // SUBTILE_MIN))
    step = tm // n
    return [slice(s * step, (s + 1) * step) for s in range(n)]


def _ln_rows(x):
    mu = jnp.mean(x, axis=-1, keepdims=True)
    xc = x - mu
    var = jnp.mean(xc * xc, axis=-1, keepdims=True)
    return xc * lax.rsqrt(var + LN_EPS)


def _sigmoid(x):
    return 1.0 / (1.0 + jnp.exp(-x))


def _dot(a, b):
    return jnp.dot(a, b, preferred_element_type=F32)


def _dot_nt(a, b):
    return lax.dot_general(a, b, (((1,), (1,)), ((), ())), preferred_element_type=F32)


def _dot_tn(a, b):
    return lax.dot_general(a, b, (((0,), (0,)), ((), ())), preferred_element_type=F32)


def _mods_kernel(c_ref, w_ref, b_ref, o_ref):
    c = c_ref[...]
    h = (c * _sigmoid(c)).astype(BF16)
    o_ref[0] = _dot(h, w_ref[0].astype(BF16)) + b_ref[0]


def _ada_mods(cvec, ada_w, ada_b):
    depth, d, nine_d = ada_w.shape
    wb = 3 * d
    nblk = nine_d // wb
    return pl.pallas_call(
        _mods_kernel,
        grid=(depth, nblk),
        in_specs=[pl.BlockSpec((8, d), lambda l, j: (0, 0)),
                  pl.BlockSpec((1, d, wb), lambda l, j: (l, 0, j)),
                  pl.BlockSpec((1, 1, wb), lambda l, j: (l, 0, j))],
        out_specs=pl.BlockSpec((1, 8, wb), lambda l, j: (l, 0, j)),
        out_shape=jax.ShapeDtypeStruct((depth, 8, nine_d), F32),
        compiler_params=_cparams(("parallel", "parallel"), 40),
        name="ada_mods",
    )(cvec, ada_w, ada_b.reshape(depth, 1, nine_d))


def _ffn_kernel(x_ref, mods_ref, w1_ref, w3_ref, w2_ref, ln_ref, o_ref, *, mod_base, ln_row, alpha):
    m = mods_ref[...]
    shift = m[mod_base:mod_base + 1]
    scale = m[mod_base + 1:mod_base + 2]
    gate = m[mod_base + 2:mod_base + 3]
    ln = ln_ref[...]
    for rows in _row_splits(x_ref.shape[0]):
        x = x_ref[rows, :]
        h = (_ln_rows(x) * (1.0 + scale) + shift).astype(BF16)
        a = _dot(h, w1_ref[...])
        b = _dot(h, w3_ref[...])
        g = (a * _sigmoid(a) * b).astype(BF16)
        y = _dot(g, w2_ref[...])
        z = alpha * x + 0.5 * gate * y
        o_ref[rows, :] = _ln_rows(z) * ln[ln_row:ln_row + 1] + ln[3 + ln_row:4 + ln_row]


def _ffn(x2, mods, mods_row, w1, w3, w2, ln, *, layer, tm, mod_base, ln_row, alpha):
    t, d = x2.shape
    dff = w1.shape[2]
    kern = functools.partial(_ffn_kernel, mod_base=mod_base, ln_row=ln_row, alpha=alpha)
    return pl.pallas_call(
        kern,
        grid=(t // tm,),
        in_specs=[pl.BlockSpec((tm, d), lambda i: (i, 0)),
                  pl.BlockSpec((None, None, 9, d), lambda i: (layer, mods_row(i), 0, 0)),
                  _resident((d, dff), layer), _resident((d, dff), layer), _resident((dff, d), layer),
                  _resident((6, d), layer)],
        out_specs=pl.BlockSpec((tm, d), lambda i: (i, 0)),
        out_shape=jax.ShapeDtypeStruct((t, d), F32),
        compiler_params=_cparams(("parallel",), 62 if tm > TOKEN_TILE else 56),
        name="ffn",
    )(x2, mods, w1, w3, w2, ln)


def _rope(t, cos, sin_signed, first_half):
    swapped = jnp.where(first_half, pltpu.roll(t, 96, 1), pltpu.roll(t, 32, 1))
    return t * cos + swapped * sin_signed


def _proj_kernel(x_ref, mods_ref, w_ref, cos_ref, sin_ref, cs_ref, *rest,
                 q_scale, nq_scale, grid_rows_out):
    extra = rest[:-10]
    q_ref, k_ref, v_ref, g_ref, nq_ref, nk_ref, nv_ref, zre_ref, zim_ref, gt_ref = rest[-10:]
    x = x_ref[...]
    m = mods_ref[...]
    h = (_ln_rows(x) * (1.0 + m[4:5]) + m[3:4]).astype(BF16)
    tm = x.shape[0]
    cw = 512

    def seg(j):
        return _dot(h, w_ref[:, j * cw:(j + 1) * cw])

    cos = cos_ref[...]
    sin = sin_ref[...]
    lane = lax.broadcasted_iota(jnp.int32, (tm, V7X_LANES), 1)
    first_half = (lane & 32) == 0

    cs = cs_ref[...]
    u = seg(9).astype(BF16)
    r = seg(0) * q_scale
    for hh in range(RET_HEADS):
        sl = slice(hh * RET_DK, (hh + 1) * RET_DK)
        q_ref[:, sl] = _rope(r[:, sl], cos, sin, first_half).astype(BF16)
    if grid_rows_out:
        perm_ref, = extra
        u = _dot(perm_ref[...], u).astype(BF16)
    r = seg(1)
    for hh in range(RET_HEADS):
        sl = slice(hh * RET_DK, (hh + 1) * RET_DK)
        k_ref[:, sl] = _rope(r[:, sl], cos, sin, first_half).astype(BF16)
    n_r = tm // GRID_W

    def fourier(gi):
        sl = slice(gi * FOU_DG, (gi + 1) * FOU_DG)
        z = _dot(u[:, sl], cs)
        if grid_rows_out:
            for c in range(GRID_W):
                dst = slice((c * FOU_GROUPS + gi) * FOU_DG, (c * FOU_GROUPS + gi + 1) * FOU_DG)
                zre_ref[:, dst] = z[c * n_r:(c + 1) * n_r, :FOU_DG]
                zim_ref[:, dst] = z[c * n_r:(c + 1) * n_r, FOU_DG:]
        else:
            zre_ref[:, sl] = z[:, :FOU_DG].astype(BF16)
            zim_ref[:, sl] = z[:, FOU_DG:].astype(BF16)

    for j in range(2):
        r = seg(4 + j)
        g_ref[:, j * cw:(j + 1) * cw] = (r * _sigmoid(r)).astype(BF16)
        fourier(j)
    for j in range(6):
        gt_ref[:, j * cw:(j + 1) * cw] = _sigmoid(seg(10 + j)).astype(BF16)
        if j < FOU_GROUPS - 2:
            fourier(2 + j)
    nq_ref[...] = (seg(6) * nq_scale).astype(BF16)
    for j in range(2):
        v_ref[:, j * cw:(j + 1) * cw] = seg(2 + j).astype(BF16)
    nk_ref[...] = seg(7).astype(BF16)
    nv_ref[...] = seg(8).astype(BF16)


def _proj(x2, mods, mods_row, w_in, cos, sin, rope_row, cs, *, layer, tm, grid_rows_out):
    t, d = x2.shape
    d_in = w_in.shape[2]
    ch = FOU_GROUPS * FOU_DG
    widths = (512, 512, 1024, 1024, 512, 512, 512, ch, ch, 3072)
    kern = functools.partial(_proj_kernel, q_scale=RET_DK ** -0.5, nq_scale=NA_DH ** -0.5 * LOG2_E,
                             grid_rows_out=grid_rows_out)
    out_specs = [pl.BlockSpec((tm, w), lambda i: (i, 0)) for w in widths]
    out_shape = [jax.ShapeDtypeStruct((t, w), BF16) for w in widths]
    in_specs = [pl.BlockSpec((tm, d), lambda i: (i, 0)),
                pl.BlockSpec((None, None, 9, d), lambda i: (layer, mods_row(i), 0, 0)),
                _resident((d, d_in), layer),
                pl.BlockSpec((tm, V7X_LANES), lambda i: (rope_row(i), 0)),
                pl.BlockSpec((tm, V7X_LANES), lambda i: (rope_row(i), 0)),
                _resident((FOU_DG, 2 * FOU_DG))]
    args = [x2, mods, w_in, cos, sin, cs]
    if grid_rows_out:
        n_r = tm // GRID_W
        for o in (7, 8):
            out_specs[o] = pl.BlockSpec((n_r, GRID_W * ch), lambda i: (i, 0))
            out_shape[o] = jax.ShapeDtypeStruct((t // GRID_W, GRID_W * ch), F32)
        src = (np.arange(n_r)[None, :] * GRID_W + np.arange(GRID_W)[:, None]).reshape(-1)
        perm = np.zeros((tm, tm), np.float32)
        perm[np.arange(tm), src] = 1.0
        in_specs.append(_resident((tm, tm)))
        args.append(jnp.asarray(perm).astype(BF16))
    return pl.pallas_call(
        kern,
        grid=(t // tm,),
        in_specs=in_specs,
        out_specs=out_specs,
        out_shape=out_shape,
        compiler_params=_cparams(("parallel",), 56),
        name="mixer_in_proj",
    )(*args)


def _log_sigmoid(x):
    return jnp.minimum(x, 0.0) - jnp.log(1.0 + jnp.exp(-jnp.abs(x)))


def _kv_update(k, v, lf, lb, idx):
    n = k.shape[0]
    kf = k.astype(F32)
    kk = jnp.concatenate([(kf * jnp.exp(lf * (n - 1.0 - idx))).astype(BF16),
                          (kf * jnp.exp(lb * idx)).astype(BF16)], axis=1)
    return _dot_tn(kk, v)


def _ret_kernel(*refs, n_chunks, chunk, with_prefix):
    if with_prefix:
        lg_ref, q_ref, k_ref, v_ref, g_ref, gn_ref, kc_ref, vc_ref, o_ref, u_ref, s_ref = refs
    else:
        lg_ref, q_ref, k_ref, v_ref, g_ref, gn_ref, o_ref, u_ref, s_ref = refs
    c = chunk
    dk, dv = RET_DK, RET_DV
    ls = _log_sigmoid(lg_ref[...])
    lf = ls[0, 0:1, :]
    lb = ls[1, 0:1, :]
    lf_v = jnp.concatenate([lf] * (dv // V7X_LANES), axis=1)
    lb_v = jnp.concatenate([lb] * (dv // V7X_LANES), axis=1)
    lf_c = jnp.concatenate([lf] * (c // V7X_LANES), axis=1)
    lb_c = jnp.concatenate([lb] * (c // V7X_LANES), axis=1)
    idx = lax.broadcasted_iota(jnp.int32, (c, V7X_LANES), 0).astype(F32)

    def upd(i, carry):
        r0 = pl.multiple_of(i * c, c)
        u_ref[i] = _kv_update(k_ref[pl.ds(r0, c), :], v_ref[pl.ds(r0, c), :], lf, lb, idx)
        return carry
    lax.fori_loop(0, n_chunks, upd, 0, unroll=min(RET_UNROLL, n_chunks))

    if with_prefix:
        n_ctx = kc_ref.shape[0]
        idx_c = lax.broadcasted_iota(jnp.int32, (n_ctx, V7X_LANES), 0).astype(F32)
        u0 = _kv_update(kc_ref[...], vc_ref[...], lf, lb, idx_c)
        sf0, sb0 = u0[:dk], u0[dk:]
    else:
        sf0 = jnp.zeros((dk, dv), F32)
        sb0 = sf0
    gf_c = jnp.exp(lf_v * float(c))
    gb_c = jnp.exp(lb_v * float(c))

    def scan_f(i, sf):
        s_ref[i, :dk, :] = sf.astype(BF16)
        return sf * gf_c + u_ref[i, :dk, :]
    lax.fori_loop(0, n_chunks, scan_f, sf0)

    def scan_b(t, sb):
        i = n_chunks - 1 - t
        s_ref[i, dk:, :] = sb.astype(BF16)
        return sb * gb_c + u_ref[i, dk:, :]
    lax.fori_loop(0, n_chunks, scan_b, sb0)

    pr = (lax.broadcasted_iota(jnp.int32, (c, c), 0)
          - lax.broadcasted_iota(jnp.int32, (c, c), 1)).astype(F32)
    dmat = jnp.where(pr >= 0.0, jnp.exp(lf_c * jnp.maximum(pr, 0.0)),
                     jnp.exp(lb_c * jnp.maximum(-pr, 0.0)))
    qdf = jnp.exp(lf * (idx + 1.0)).astype(BF16)
    qdb = jnp.exp(lb * (float(c) - idx)).astype(BF16)
    gn = gn_ref[...].astype(BF16)
    gn_w, gn_b = gn[0:1], gn[1:2]

    def out(i, carry):
        r0 = pl.multiple_of(i * c, c)
        q = q_ref[pl.ds(r0, c), :]
        k = k_ref[pl.ds(r0, c), :]
        v = v_ref[pl.ds(r0, c), :]
        inner = (_dot_nt(q, k) * dmat).astype(BF16)
        qq = jnp.concatenate([q * qdf, q * qdb], axis=1)
        o = _dot(inner, v) + _dot(qq, s_ref[i])
        o_ref[pl.ds(r0, c), :] = (_ln_rows(o).astype(BF16) * gn_w + gn_b) * g_ref[pl.ds(r0, c), :]
        return carry
    lax.fori_loop(0, n_chunks, out, 0, unroll=min(RET_UNROLL, n_chunks))


def _retention(lg, q, k, v, g, gn, kc=None, vc=None, *, layer):
    b, n, _ = q.shape
    c = min(RET_CHUNK, n)
    nc = n // c
    with_prefix = kc is not None
    kern = functools.partial(_ret_kernel, n_chunks=nc, chunk=c, with_prefix=with_prefix)
    in_specs = [pl.BlockSpec((None, None, 2, 8, V7X_LANES), lambda bi, hi: (layer, hi, 0, 0, 0)),
                pl.BlockSpec((None, n, RET_DK), lambda bi, hi: (bi, 0, hi)),
                pl.BlockSpec((None, n, RET_DK), lambda bi, hi: (bi, 0, hi)),
                pl.BlockSpec((None, n, RET_DV), lambda bi, hi: (bi, 0, hi)),
                pl.BlockSpec((None, n, RET_DV), lambda bi, hi: (bi, 0, hi)),
                pl.BlockSpec((None, None, 2, RET_DV), lambda bi, hi: (layer, hi, 0, 0))]
    args = [lg, q, k, v, g, gn]
    if with_prefix:
        n_ctx = kc.shape[1]
        in_specs += [pl.BlockSpec((None, n_ctx, RET_DK), lambda bi, hi: (bi, 0, hi)),
                     pl.BlockSpec((None, n_ctx, RET_DV), lambda bi, hi: (bi, 0, hi))]
        args += [kc, vc]
    return pl.pallas_call(
        kern,
        grid=(b, RET_HEADS),
        in_specs=in_specs,
        out_specs=pl.BlockSpec((None, n, RET_DV), lambda bi, hi: (bi, 0, hi)),
        out_shape=jax.ShapeDtypeStruct((b, n, RET_HEADS * RET_DV), BF16),
        scratch_shapes=[pltpu.VMEM((nc, 2 * RET_DK, RET_DV), F32),
                        pltpu.VMEM((nc, 2 * RET_DK, RET_DV), BF16)],
        compiler_params=_cparams(("parallel", "parallel"), 56),
        name="retention",
    )(*args)


def _softmax_pv(s_parts, v_parts):
    m = None
    for s in s_parts:
        mi = jnp.max(s, axis=-1, keepdims=True)
        m = mi if m is None else jnp.maximum(m, mi)
    l = None
    o = None
    for s, v in zip(s_parts, v_parts):
        p = jnp.exp2(s - m)
        li = jnp.sum(p, axis=-1, keepdims=True)
        oi = _dot(p.astype(BF16), v)
        l = li if l is None else l + li
        o = oi if o is None else o + oi
    return o / l


def _na_kernel(q_ref, k_ref, v_ref, kc_ref, vc_ref, bias_ref, o_ref, s_ref, *, n_groups, key_row0_max):
    gq = NA_GROUP_ROWS * GRID_W
    nk = NA_KEY_ROWS * GRID_W
    lane = lax.broadcasted_iota(jnp.int32, (gq, V7X_LANES), 1)
    head0 = lane < NA_DH
    kc = kc_ref[...]
    vc = vc_ref[...]

    def key_start(g):
        kb = jnp.clip(g * NA_GROUP_ROWS - NA_KH // 2, 0, key_row0_max)
        return pl.multiple_of(kb * GRID_W, GRID_W)

    def scores(g, slot):
        q0 = pl.multiple_of(g * gq, gq)
        cfg = jnp.where(g == 0, 0, jnp.where(g == n_groups - 1, 2, 1))
        q = q_ref[pl.ds(q0, gq), :]
        kl = k_ref[pl.ds(key_start(g), nk), :]
        zero = jnp.zeros_like(q)
        for hh in range(2):
            qh = jnp.where(head0, q, zero) if hh == 0 else jnp.where(head0, zero, q)
            s_ref[slot, hh, :, :nk] = _dot_nt(qh, kl) + bias_ref[cfg, hh]
            s_ref[slot, hh, :, nk:] = _dot_nt(qh, kc)

    def finish(g, slot):
        q0 = pl.multiple_of(g * gq, gq)
        vl = v_ref[pl.ds(key_start(g), nk), :]
        outs = [_softmax_pv([s_ref[slot, hh, :, :nk], s_ref[slot, hh, :, nk:]], [vl, vc])
                for hh in range(2)]
        o_ref[pl.ds(q0, gq), :] = jnp.where(head0, outs[0], outs[1]).astype(BF16)

    scores(0, 0)

    def pair(j, carry):
        g = 2 * j
        scores(g + 1, 1)
        finish(g, 0)
        scores(jnp.minimum(g + 2, n_groups - 1), 0)
        finish(g + 1, 1)
        return carry
    lax.fori_loop(0, n_groups // 2, pair, 0, unroll=2)


def _na_bias_tables(rpb):
    h = rpb.shape[0]
    w = GRID_W
    cols = np.arange(w)
    cs = np.clip(cols - NA_KW // 2, 0, w - NA_KW)
    kcol = np.arange(w)[None, :]
    colmask = (kcol >= cs[:, None]) & (kcol < cs[:, None] + NA_KW)
    pad = w - 1
    off = pad + NA_KW - 1
    period = off + w + 1
    rpb_p = jnp.pad(rpb, ((0, 0), (0, 0), (pad, period - pad - rpb.shape[2])))
    skew = jnp.tile(rpb_p, (1, 1, w))[:, :, :w * (period - 1)].reshape(h, -1, w, period - 1)
    toep = skew[:, :, :, off:off + w]
    toep = jnp.where(colmask[None, None], toep * LOG2_E, NEG_BIG)
    n_dr = toep.shape[1]
    toep = jnp.concatenate([toep, jnp.full((h, 1, w, w), NEG_BIG, F32)], axis=1)
    dr = np.full((3, NA_GROUP_ROWS, NA_KEY_ROWS), n_dr, np.int32)
    for cfg in range(3):
        rel = NA_GROUP_ROWS * cfg
        for rr in range(NA_GROUP_ROWS):
            lo = (0, rr, NA_KEY_ROWS - NA_KH)[cfg]
            for i in range(lo, lo + NA_KH):
                dr[cfg, rr, i] = i - rel - rr + NA_KH - 1
    pairs = dr.reshape(3, NA_GROUP_ROWS, NA_KEY_ROWS // 2, 2)
    uniq = sorted({(int(a), int(c)) for a, c in pairs.reshape(-1, 2)})
    slot_of = {p: u for u, p in enumerate(uniq)}
    piece = np.array([[[slot_of[(int(a), int(c))] for a, c in row] for row in cfg] for cfg in pairs])
    left = jnp.take(toep, jnp.asarray([p[0] for p in uniq]), axis=1)
    right = jnp.take(toep, jnp.asarray([p[1] for p in uniq]), axis=1)
    pieces = jnp.concatenate([left, right], axis=-1)
    n_u = len(uniq)

    def assemble(p_ref, o_ref):
        for cfg in range(3):
            for rr in range(NA_GROUP_ROWS):
                for pc in range(NA_KEY_ROWS // 2):
                    o_ref[cfg, rr * w:(rr + 1) * w, pc * 2 * w:(pc + 1) * 2 * w] = (
                        p_ref[int(piece[cfg, rr, pc])])

    return pl.pallas_call(
        assemble,
        grid=(h,),
        in_specs=[pl.BlockSpec((None, n_u, w, 2 * w), lambda hi: (hi, 0, 0, 0))],
        out_specs=pl.BlockSpec((3, None, NA_GROUP_ROWS * w, NA_KEY_ROWS * w),
                               lambda hi: (0, hi, 0, 0)),
        out_shape=jax.ShapeDtypeStruct((3, h, NA_GROUP_ROWS * w, NA_KEY_ROWS * w), F32),
        compiler_params=_cparams(("parallel",), 32),
        name="na_bias_table",
    )(pieces)


def _neighbourhood_attention(nq, nk, nv, kc, vc, bias):
    b, n, hw = nq.shape
    rows = n // GRID_W
    n_ctx = kc.shape[1]
    n_groups = rows // NA_GROUP_ROWS
    pairs = NA_HEADS // 2
    gq = NA_GROUP_ROWS * GRID_W
    nkeys = NA_KEY_ROWS * GRID_W
    kern = functools.partial(_na_kernel, n_groups=n_groups, key_row0_max=rows - NA_KEY_ROWS)
    tok = pl.BlockSpec((None, n, V7X_LANES), lambda bi, pi: (bi, 0, pi))
    ctx = pl.BlockSpec((None, n_ctx, V7X_LANES), lambda bi, pi: (bi, 0, pi))
    return pl.pallas_call(
        kern,
        grid=(b, pairs),
        in_specs=[tok, tok, tok, ctx, ctx,
                  pl.BlockSpec((3, 2, gq, nkeys), lambda bi, pi: (0, pi, 0, 0))],
        out_specs=tok,
        out_shape=jax.ShapeDtypeStruct((b, n, hw), BF16),
        scratch_shapes=[pltpu.VMEM((2, 2, gq, nkeys + n_ctx), F32)],
        compiler_params=_cparams(("parallel", "parallel"), 48),
        name="neighbourhood_attention",
    )(nq, nk, nv, kc, vc, bias)


def _ctx_attn_kernel(q_ref, k_ref, v_ref, o_ref):
    q = q_ref[...]
    k = k_ref[...]
    v = v_ref[...]
    lane = lax.broadcasted_iota(jnp.int32, q.shape, 1)
    head0 = lane < NA_DH
    zero = jnp.zeros_like(q)
    outs = []
    for hh in range(2):
        qh = jnp.where(head0, q, zero) if hh == 0 else jnp.where(head0, zero, q)
        outs.append(_softmax_pv([_dot_nt(qh, k)], [v]))
    o_ref[...] = jnp.where(head0, outs[0], outs[1]).astype(BF16)


def _context_attention(q, k, v):
    b, n_ctx, hw = q.shape
    blk = pl.BlockSpec((None, n_ctx, V7X_LANES), lambda bi, pi: (bi, 0, pi))
    return pl.pallas_call(
        _ctx_attn_kernel,
        grid=(b, NA_HEADS // 2),
        in_specs=[blk, blk, blk],
        out_specs=blk,
        out_shape=jax.ShapeDtypeStruct((b, n_ctx, hw), BF16),
        compiler_params=_cparams(("parallel", "parallel"), 32),
        name="context_attention",
    )(q, k, v)


def _dft_rows_kernel(zre_ref, zim_ref, w_ref, are_ref, aim_ref):
    r = zre_ref.shape[0]
    z = jnp.concatenate([zre_ref[...].astype(BF16), zim_ref[...].astype(BF16)], axis=0)
    a = _dot(w_ref[...], z)
    are_ref[...] = a[:r].astype(BF16)
    aim_ref[...] = a[r:].astype(BF16)


def _dft_rows(zre, zim, w2):
    b, r, wc = zre.shape
    cb = min(4096, wc)
    blk = pl.BlockSpec((None, r, cb), lambda bi, ci: (bi, 0, ci))
    return pl.pallas_call(
        _dft_rows_kernel,
        grid=(b, wc // cb),
        in_specs=[blk, blk, _resident((2 * r, 2 * r))],
        out_specs=[blk, blk],
        out_shape=[jax.ShapeDtypeStruct((b, r, wc), BF16)] * 2,
        compiler_params=_cparams(("parallel", "parallel"), 40),
        name="dft_rows",
    )(zre, zim, w2)


def _dft_cols_kernel(are_ref, aim_ref, twc_ref, tws_ref, km_ref, wo_ref, o_ref, *, scale, ch, jb):
    w = are_ref.shape[1] // ch
    reps = ch // V7X_LANES
    slabs_re = [are_ref[:, c * ch:(c + 1) * ch].astype(F32) for c in range(w)]
    slabs_im = [aim_ref[:, c * ch:(c + 1) * ch].astype(F32) for c in range(w)]
    for hf in range(are_ref.shape[0] // jb):
        rs = slice(hf * jb, (hf + 1) * jb)
        ts_rows = slice(hf * jb * w, (hf + 1) * jb * w)
        ar = jnp.concatenate([s[rs] for s in slabs_re], axis=0)
        ai = jnp.concatenate([s[rs] for s in slabs_im], axis=0)
        tc = jnp.concatenate([twc_ref[ts_rows, :]] * reps, axis=1)
        ts = jnp.concatenate([tws_ref[ts_rows, :]] * reps, axis=1)
        a = jnp.concatenate([(ar * tc - ai * ts).astype(BF16), (ar * ts + ai * tc).astype(BF16)],
                            axis=0)
        y = _dot(km_ref[...], a) * scale
        out = _dot(y.astype(BF16), wo_ref[...])
        o_ref[:, rs, :] = out.reshape(w, jb, out.shape[1])


def _dft_cols(a_re, a_im, twc, tws, kmat, w_out, *, layer, scale):
    b, r, wc = a_re.shape
    w = GRID_W
    ch = wc // w
    d = w_out.shape[2]
    jb = FOU_K1_BLOCK
    ab = FOU_A_ROWS
    blk = pl.BlockSpec((None, ab, wc), lambda bi, ji: (bi, ji, 0))
    tw = pl.BlockSpec((ab * w, V7X_LANES), lambda bi, ji: (ji, 0))
    kern = functools.partial(_dft_cols_kernel, scale=scale, ch=ch, jb=jb)
    return pl.pallas_call(
        kern,
        grid=(b, r // ab),
        in_specs=[blk, blk, tw, tw, _resident(kmat.shape), _resident((ch, d), layer)],
        out_specs=pl.BlockSpec((None, w, ab, d), lambda bi, ji: (bi, 0, ji, 0)),
        out_shape=jax.ShapeDtypeStruct((b, w, r, d), F32),
        compiler_params=_cparams(("parallel", "parallel"), 40),
        name="dft_cols",
    )(a_re, a_im, twc, tws, kmat, w_out)


def _ctx_fourier_kernel(zre_ref, zim_ref, cm_ref, wo_ref, o_ref, *, scale):
    z = jnp.concatenate([zre_ref[...], zim_ref[...]], axis=0)
    y = _dot(cm_ref[...], z) * scale
    o_ref[...] = _dot(y.astype(BF16), wo_ref[...])


def _ctx_fourier(zre, zim, cmat, w_out, *, layer, scale):
    b, n_ctx, ch = zre.shape
    d = w_out.shape[2]
    blk = pl.BlockSpec((None, n_ctx, ch), lambda bi: (bi, 0, 0))
    kern = functools.partial(_ctx_fourier_kernel, scale=scale)
    return pl.pallas_call(
        kern,
        grid=(b,),
        in_specs=[blk, blk, _resident(cmat.shape), _resident((ch, d), layer)],
        out_specs=pl.BlockSpec((None, n_ctx, d), lambda bi: (bi, 0, 0)),
        out_shape=jax.ShapeDtypeStruct((b, n_ctx, d), F32),
        compiler_params=_cparams(("parallel",), 32),
        name="ctx_fourier",
    )(zre, zim, cmat, w_out)


def _merge_kernel(x_ref, mods_ref, ret_ref, na_ref, fou_ref, gt_ref, wr_ref, wn_ref, wo_ref, ln_ref,
                  o_ref, *, alpha):
    d = x_ref.shape[1]
    m = mods_ref[...]
    ln = ln_ref[...]
    for rows in _row_splits(x_ref.shape[0]):
        y_ret = _dot(ret_ref[rows, :], wr_ref[...])
        y_na = _dot(na_ref[rows, :], wn_ref[...])
        mix = (gt_ref[rows, 0:d].astype(F32) * y_ret + gt_ref[rows, d:2 * d].astype(F32) * y_na
               + gt_ref[rows, 2 * d:3 * d].astype(F32) * fou_ref[rows, :])
        y = _dot(mix.astype(BF16), wo_ref[...])
        z = alpha * x_ref[rows, :] + m[5:6] * y
        o_ref[rows, :] = _ln_rows(z) * ln[1:2] + ln[4:5]


def _merge(x2, mods, mods_row, ret, na, fou, gt, wr, wn, wo, ln, *, layer, tm, alpha):
    t, d = x2.shape
    kern = functools.partial(_merge_kernel, alpha=alpha)

    def rows(wd):
        return pl.BlockSpec((tm, wd), lambda i: (i, 0))
    return pl.pallas_call(
        kern,
        grid=(t // tm,),
        in_specs=[rows(d), pl.BlockSpec((None, None, 9, d), lambda i: (layer, mods_row(i), 0, 0)),
                  rows(ret.shape[1]), rows(na.shape[1]), rows(d), rows(gt.shape[1]),
                  _resident(wr.shape[1:], layer), _resident(wn.shape[1:], layer),
                  _resident(wo.shape[1:], layer), _resident((6, d), layer)],
        out_specs=rows(d),
        out_shape=jax.ShapeDtypeStruct((t, d), F32),
        compiler_params=_cparams(("parallel",), 60 if tm > TOKEN_TILE else 48),
        name="merge",
    )(x2, mods, ret, na, fou, gt, wr, wn, wo, ln)


def _rope_tables(n):
    half = RET_DK // 4
    t = np.arange(n)
    row, col = t // GRID_W, t % GRID_W
    freqs = ROPE_BASE ** (-np.arange(half, dtype=np.float64) / half)
    ang_r = row.astype(np.float64)[:, None] * freqs[None, :]
    ang_c = col.astype(np.float64)[:, None] * freqs[None, :]
    cos = np.concatenate([np.cos(ang_r)] * 2 + [np.cos(ang_c)] * 2, axis=1)
    sin = np.concatenate([-np.sin(ang_r), np.sin(ang_r), -np.sin(ang_c), np.sin(ang_c)], axis=1)
    return jnp.asarray(cos, F32), jnp.asarray(sin, F32)


def _dft_tables(n):
    w = GRID_W
    r = n // w
    jb = FOU_K1_BLOCK
    two_pi = 2.0 * np.pi
    ch = np.arange(FOU_DG)
    ang = two_pi * ((ch[:, None] * ch[None, :]) % FOU_DG) / FOU_DG
    cs = np.concatenate([np.cos(ang), np.sin(ang)], axis=1)
    k1 = np.arange(r)
    ang = two_pi * ((k1[:, None] * k1[None, :]) % r) / r
    w2 = np.block([[np.cos(ang), -np.sin(ang)], [np.sin(ang), np.cos(ang)]])
    c = np.arange(w)
    k1_blocks = k1.reshape(r // jb, 1, jb)
    ang = two_pi * (k1_blocks * c[None, :, None]).reshape(-1) / n
    twc = np.repeat(np.cos(ang)[:, None], V7X_LANES, axis=1)
    tws = np.repeat(np.sin(ang)[:, None], V7X_LANES, axis=1)
    ang = two_pi * ((c[:, None] * c[None, :]) % w) / w
    eye = np.eye(jb)
    kc = np.einsum("kc,ji->kjci", np.cos(ang), eye).reshape(w * jb, w * jb)
    ks = np.einsum("kc,ji->kjci", np.sin(ang), eye).reshape(w * jb, w * jb)
    kmat = np.concatenate([kc, -ks], axis=1)
    as_bf16 = lambda a: jnp.asarray(a, F32).astype(BF16)
    return (as_bf16(cs), as_bf16(w2), jnp.asarray(twc, F32), jnp.asarray(tws, F32), as_bf16(kmat))


def _ctx_dft_matrix(n_ctx):
    m = np.arange(n_ctx)
    ang = 2.0 * np.pi * ((m[:, None] * m[None, :]) % n_ctx) / n_ctx
    return jnp.asarray(np.concatenate([np.cos(ang), -np.sin(ang)], axis=1), F32).astype(BF16)


def kernel(x, c, ctx, c_ctx, ada_w, ada_b, ln_w, ln_b, ffa_w1, ffa_w3, ffa_w2, mix_w_in,
           ret_decay_logit, ret_gn_w, ret_gn_b, ret_w_out, na_rpb, na_w_out, fou_w_out, mix_w_o,
           ffb_w1, ffb_w3, ffb_w2):
    b, n, d = x.shape
    n_ctx = ctx.shape[1]
    depth = ada_w.shape[0]
    rows = n // GRID_W
    assert n % TOKEN_TILE == 0 and rows % (2 * NA_GROUP_ROWS) == 0 and rows >= NA_KEY_ROWS
    assert rows % FOU_A_ROWS == 0 and b + 1 <= 8
    alpha = (2 * depth) ** 0.25
    tm = TOKEN_TILE
    tm_ffn = FFN_TILE if n % FFN_TILE == 0 else TOKEN_TILE
    tiles_per_seq = n // tm
    lat_row = lambda i: i // tiles_per_seq
    ffn_row = lambda i: i // (n // tm_ffn)
    ctx_row = lambda i: b
    lat_rope = lambda i: i % tiles_per_seq
    ctx_rope = lambda i: 0
    t_ctx = b * n_ctx
    tc = TOKEN_TILE if t_ctx % TOKEN_TILE == 0 else n_ctx
    tc_ffn = FFN_TILE if t_ctx % FFN_TILE == 0 else tc

    cos_x, sin_x = _rope_tables(n)
    cos_c = jnp.ones((tc, V7X_LANES), F32)
    sin_c = jnp.zeros((tc, V7X_LANES), F32)
    cs, w2, twc, tws, kmat = _dft_tables(n)
    cmat_ctx = _ctx_dft_matrix(n_ctx)
    fou_scale_x = 1.0 / math.sqrt(n * FOU_DG)
    fou_scale_c = 1.0 / math.sqrt(n_ctx * FOU_DG)

    cvec = jnp.concatenate([c, c_ctx[None, :], jnp.zeros((8 - b - 1, d), F32)], axis=0)
    mods_all = _ada_mods(cvec, ada_w, ada_b).reshape(depth, 8, 9, d)
    ln_all = jnp.concatenate([ln_w, ln_b], axis=1)
    to_bf16 = lambda a: a.astype(BF16)
    ffa = [to_bf16(a) for a in (ffa_w1, ffa_w3, ffa_w2)]
    ffb = [to_bf16(a) for a in (ffb_w1, ffb_w3, ffb_w2)]
    w_in, wr, wn, wf, wo = [to_bf16(a) for a in (mix_w_in, ret_w_out, na_w_out, fou_w_out, mix_w_o)]
    lg_all = jnp.broadcast_to(
        jnp.transpose(ret_decay_logit, (0, 2, 1))[:, :, :, None, None],
        (depth, RET_HEADS, 2, 8, V7X_LANES)).astype(F32)
    gn_all = jnp.stack([ret_gn_w.reshape(depth, RET_HEADS, RET_DV),
                        ret_gn_b.reshape(depth, RET_HEADS, RET_DV)], axis=2)

    xs = x.reshape(b * n, d)
    cx = ctx.reshape(b * n_ctx, d)
    mods, ln = mods_all, ln_all
    for l in range(depth):
        last = l == depth - 1
        xs = _ffn(xs, mods, ffn_row, *ffa, ln, layer=l, tm=tm_ffn, mod_base=0, ln_row=0, alpha=alpha)
        cx = _ffn(cx, mods, ctx_row, *ffa, ln, layer=l, tm=tc_ffn, mod_base=0, ln_row=0, alpha=alpha)

        px = _proj(xs, mods, lat_row, w_in, cos_x, sin_x, lat_rope, cs, layer=l, tm=tm,
                   grid_rows_out=True)
        pc = _proj(cx, mods, ctx_row, w_in, cos_c, sin_c, ctx_rope, cs, layer=l, tm=tc,
                   grid_rows_out=False)
        q_x, k_x, v_x, g_x, nq_x, nk_x, nv_x = [a.reshape(b, n, -1) for a in px[:7]]
        gt_x = px[9]
        q_c, k_c, v_c, g_c, nq_c, nk_c, nv_c, zre_c, zim_c, gt_c = [a.reshape(b, n_ctx, -1) for a in pc]

        ret_x = _retention(lg_all, q_x, k_x, v_x, g_x, gn_all, k_c, v_c, layer=l)
        bias = _na_bias_tables(na_rpb[l])
        na_x = _neighbourhood_attention(nq_x, nk_x, nv_x, nk_c, nv_c, bias)
        a_re, a_im = _dft_rows(px[7].reshape(b, rows, -1), px[8].reshape(b, rows, -1), w2)
        fou_x = _dft_cols(a_re, a_im, twc, tws, kmat, wf, layer=l, scale=fou_scale_x)
        xs = _merge(xs, mods, ffn_row, ret_x.reshape(b * n, -1), na_x.reshape(b * n, -1),
                    fou_x.reshape(b * n, d), gt_x, wr, wn, wo, ln,
                    layer=l, tm=tm_ffn, alpha=alpha)
        xs = _ffn(xs, mods, ffn_row, *ffb, ln, layer=l, tm=tm_ffn, mod_base=6, ln_row=2, alpha=alpha)
        if not last:
            ret_c = _retention(lg_all, q_c, k_c, v_c, g_c, gn_all, layer=l)
            na_c = _context_attention(nq_c, nk_c, nv_c)
            fou_c = _ctx_fourier(zre_c, zim_c, cmat_ctx, wf, layer=l, scale=fou_scale_c)
            cx = _merge(cx, mods, ctx_row, ret_c.reshape(b * n_ctx, -1), na_c.reshape(b * n_ctx, -1),
                        fou_c.reshape(b * n_ctx, d), gt_c.reshape(b * n_ctx, -1), wr, wn, wo, ln,
                        layer=l, tm=tc_ffn, alpha=alpha)
            cx = _ffn(cx, mods, ctx_row, *ffb, ln, layer=l, tm=tc_ffn, mod_base=6, ln_row=2,
                      alpha=alpha)
    return xs.reshape(b, n, d)
```

```python
import functools
import math

import numpy as np
import jax
import jax.numpy as jnp
from jax import lax
from jax.experimental import pallas as pl
from jax.experimental.pallas import tpu as pltpu

F32 = jnp.float32
BF16 = jnp.bfloat16

GRID_W = 64
RET_HEADS = 4
RET_DK = 128
RET_DV = 256
NA_HEADS = 8
NA_DH = 64
NA_KH = 8
NA_KW = 16
FOU_GROUPS = 4
FOU_DG = 128
N_BRANCH = 3
ROPE_BASE = 10000.0
LN_EPS = 1e-6

V7X_VMEM_BYTES = 64 * 1024 * 1024
V7X_LANES = 128
V7X_SUBLANES = 8

TOKEN_TILE = 512
RET_CHUNK = 256
FFN_TILE = 1024
TOKEN_SPLIT = 4
SUBTILE_MIN = 256
RET_UNROLL = 8
NA_GROUP_ROWS = 4
NA_KEY_ROWS = NA_GROUP_ROWS + NA_KH
FOU_K1_BLOCK = V7X_SUBLANES
FOU_A_ROWS = 2 * V7X_SUBLANES
NEG_BIG = -1e30
LOG2_E = math.log2(math.e)


def _cparams(sem, vmem_mb):
    return pltpu.CompilerParams(dimension_semantics=sem,
                                vmem_limit_bytes=int(vmem_mb * 1024 * 1024))


def _resident(shape, layer=None):
    nd = len(shape)
    if layer is None:
        return pl.BlockSpec(shape, lambda *_: (0,) * nd, pipeline_mode=pl.Buffered(1))
    return pl.BlockSpec((None,) + tuple(shape), lambda *_: (layer,) + (0,) * nd,
                        pipeline_mode=pl.Buffered(1))


def _row_splits(tm):
    n = max(1, min(TOKEN_SPLIT, tm // SUBTILE_MIN))
    step = tm // n
    return [slice(s * step, (s + 1) * step) for s in range(n)]


def _ln_rows(x):
    mu = jnp.mean(x, axis=-1, keepdims=True)
    xc = x - mu
    var = jnp.mean(xc * xc, axis=-1, keepdims=True)
    return xc * lax.rsqrt(var + LN_EPS)


def _sigmoid(x):
    return 1.0 / (1.0 + jnp.exp(-x))


def _dot(a, b):
    return jnp.dot(a, b, preferred_element_type=F32)


def _dot_nt(a, b):
    return lax.dot_general(a, b, (((1,), (1,)), ((), ())), preferred_element_type=F32)


def _dot_tn(a, b):
    return lax.dot_general(a, b, (((0,), (0,)), ((), ())), preferred_element_type=F32)


def _mods_kernel(c_ref, w_ref, b_ref, o_ref):
    c = c_ref[...]
    h = (c * _sigmoid(c)).astype(BF16)
    o_ref[0] = _dot(h, w_ref[0].astype(BF16)) + b_ref[0]


def _ada_mods(cvec, ada_w, ada_b):
    depth, d, nine_d = ada_w.shape
    wb = 3 * d
    nblk = nine_d // wb
    return pl.pallas_call(
        _mods_kernel,
        grid=(depth, nblk),
        in_specs=[pl.BlockSpec((8, d), lambda l, j: (0, 0)),
                  pl.BlockSpec((1, d, wb), lambda l, j: (l, 0, j)),
                  pl.BlockSpec((1, 1, wb), lambda l, j: (l, 0, j))],
        out_specs=pl.BlockSpec((1, 8, wb), lambda l, j: (l, 0, j)),
        out_shape=jax.ShapeDtypeStruct((depth, 8, nine_d), F32),
        compiler_params=_cparams(("parallel", "parallel"), 40),
        name="ada_mods",
    )(cvec, ada_w, ada_b.reshape(depth, 1, nine_d))


def _ffn_kernel(x_ref, mods_ref, w1_ref, w3_ref, w2_ref, ln_ref, o_ref, *, mod_base, ln_row, alpha):
    m = mods_ref[...]
    shift = m[mod_base:mod_base + 1]
    scale = m[mod_base + 1:mod_base + 2]
    gate = m[mod_base + 2:mod_base + 3]
    ln = ln_ref[...]
    for rows in _row_splits(x_ref.shape[0]):
        x = x_ref[rows, :]
        h = (_ln_rows(x) * (1.0 + scale) + shift).astype(BF16)
        a = _dot(h, w1_ref[...])
        b = _dot(h, w3_ref[...])
        g = (a * _sigmoid(a) * b).astype(BF16)
        y = _dot(g, w2_ref[...])
        z = alpha * x + 0.5 * gate * y
        o_ref[rows, :] = _ln_rows(z) * ln[ln_row:ln_row + 1] + ln[3 + ln_row:4 + ln_row]


def _ffn(x2, mods, mods_row, w1, w3, w2, ln, *, layer, tm, mod_base, ln_row, alpha):
    t, d = x2.shape
    dff = w1.shape[2]
    kern = functools.partial(_ffn_kernel, mod_base=mod_base, ln_row=ln_row, alpha=alpha)
    return pl.pallas_call(
        kern,
        grid=(t // tm,),
        in_specs=[pl.BlockSpec((tm, d), lambda i: (i, 0)),
                  pl.BlockSpec((None, None, 9, d), lambda i: (layer, mods_row(i), 0, 0)),
                  _resident((d, dff), layer), _resident((d, dff), layer), _resident((dff, d), layer),
                  _resident((6, d), layer)],
        out_specs=pl.BlockSpec((tm, d), lambda i: (i, 0)),
        out_shape=jax.ShapeDtypeStruct((t, d), F32),
        compiler_params=_cparams(("parallel",), 62 if tm > TOKEN_TILE else 56),
        name="ffn",
    )(x2, mods, w1, w3, w2, ln)


def _rope(t, cos, sin_signed, first_half):
    swapped = jnp.where(first_half, pltpu.roll(t, 96, 1), pltpu.roll(t, 32, 1))
    return t * cos + swapped * sin_signed


def _proj_kernel(x_ref, mods_ref, w_ref, cos_ref, sin_ref, cs_ref, *rest,
                 q_scale, nq_scale, grid_rows_out):
    extra = rest[:-10]
    q_ref, k_ref, v_ref, g_ref, nq_ref, nk_ref, nv_ref, zre_ref, zim_ref, gt_ref = rest[-10:]
    m = mods_ref[...]
    tm = x_ref.shape[0]
    cw = 512
    hs = [(_ln_rows(x_ref[rows, :]) * (1.0 + m[4:5]) + m[3:4]).astype(BF16)
          for rows in _row_splits(tm)]

    def seg(j):
        wj = w_ref[:, j * cw:(j + 1) * cw]
        return jnp.concatenate([_dot(h, wj) for h in hs], axis=0)

    cos = cos_ref[...]
    sin = sin_ref[...]
    lane = lax.broadcasted_iota(jnp.int32, (tm, V7X_LANES), 1)
    first_half = (lane & 32) == 0

    cs = cs_ref[...]
    u = seg(9).astype(BF16)
    r = seg(0) * q_scale
    for hh in range(RET_HEADS):
        sl = slice(hh * RET_DK, (hh + 1) * RET_DK)
        q_ref[:, sl] = _rope(r[:, sl], cos, sin, first_half).astype(BF16)
    if grid_rows_out:
        perm_ref, = extra
        u = _dot(perm_ref[...], u).astype(BF16)
    r = seg(1)
    for hh in range(RET_HEADS):
        sl = slice(hh * RET_DK, (hh + 1) * RET_DK)
        k_ref[:, sl] = _rope(r[:, sl], cos, sin, first_half).astype(BF16)
    n_r = tm // GRID_W

    def fourier(gi):
        sl = slice(gi * FOU_DG, (gi + 1) * FOU_DG)
        z = _dot(u[:, sl], cs)
        if grid_rows_out:
            for c in range(GRID_W):
                dst = slice((c * FOU_GROUPS + gi) * FOU_DG, (c * FOU_GROUPS + gi + 1) * FOU_DG)
                zre_ref[:, dst] = z[c * n_r:(c + 1) * n_r, :FOU_DG]
                zim_ref[:, dst] = z[c * n_r:(c + 1) * n_r, FOU_DG:]
        else:
            zre_ref[:, sl] = z[:, :FOU_DG].astype(BF16)
            zim_ref[:, sl] = z[:, FOU_DG:].astype(BF16)

    for j in range(2):
        r = seg(4 + j)
        g_ref[:, j * cw:(j + 1) * cw] = (r * _sigmoid(r)).astype(BF16)
        fourier(j)
    for j in range(6):
        gt_ref[:, j * cw:(j + 1) * cw] = _sigmoid(seg(10 + j)).astype(BF16)
        if j < FOU_GROUPS - 2:
            fourier(2 + j)
    nq_ref[...] = (seg(6) * nq_scale).astype(BF16)
    for j in range(2):
        v_ref[:, j * cw:(j + 1) * cw] = seg(2 + j).astype(BF16)
    nk_ref[...] = seg(7).astype(BF16)
    nv_ref[...] = seg(8).astype(BF16)


def _proj(x2, mods, mods_row, w_in, cos, sin, rope_row, cs, *, layer, tm, grid_rows_out):
    t, d = x2.shape
    d_in = w_in.shape[2]
    ch = FOU_GROUPS * FOU_DG
    widths = (512, 512, 1024, 1024, 512, 512, 512, ch, ch, 3072)
    kern = functools.partial(_proj_kernel, q_scale=RET_DK ** -0.5, nq_scale=NA_DH ** -0.5 * LOG2_E,
                             grid_rows_out=grid_rows_out)
    out_specs = [pl.BlockSpec((tm, w), lambda i: (i, 0)) for w in widths]
    out_shape = [jax.ShapeDtypeStruct((t, w), BF16) for w in widths]
    in_specs = [pl.BlockSpec((tm, d), lambda i: (i, 0)),
                pl.BlockSpec((None, None, 9, d), lambda i: (layer, mods_row(i), 0, 0)),
                _resident((d, d_in), layer),
                pl.BlockSpec((tm, V7X_LANES), lambda i: (rope_row(i), 0)),
                pl.BlockSpec((tm, V7X_LANES), lambda i: (rope_row(i), 0)),
                _resident((FOU_DG, 2 * FOU_DG))]
    args = [x2, mods, w_in, cos, sin, cs]
    if grid_rows_out:
        n_r = tm // GRID_W
        for o in (7, 8):
            out_specs[o] = pl.BlockSpec((n_r, GRID_W * ch), lambda i: (i, 0))
            out_shape[o] = jax.ShapeDtypeStruct((t // GRID_W, GRID_W * ch), F32)
        src = (np.arange(n_r)[None, :] * GRID_W + np.arange(GRID_W)[:, None]).reshape(-1)
        perm = np.zeros((tm, tm), np.float32)
        perm[np.arange(tm), src] = 1.0
        in_specs.append(_resident((tm, tm)))
        args.append(jnp.asarray(perm).astype(BF16))
    return pl.pallas_call(
        kern,
        grid=(t // tm,),
        in_specs=in_specs,
        out_specs=out_specs,
        out_shape=out_shape,
        compiler_params=_cparams(("parallel",), 56),
        name="mixer_in_proj",
    )(*args)


def _log_sigmoid(x):
    return jnp.minimum(x, 0.0) - jnp.log(1.0 + jnp.exp(-jnp.abs(x)))


def _kv_update(k, v, lf, lb, idx):
    n = k.shape[0]
    kf = k.astype(F32)
    kk = jnp.concatenate([(kf * jnp.exp(lf * (n - 1.0 - idx))).astype(BF16),
                          (kf * jnp.exp(lb * idx)).astype(BF16)], axis=1)
    return _dot_tn(kk, v)


def _ret_kernel(*refs, n_chunks, chunk, with_prefix):
    if with_prefix:
        lg_ref, q_ref, k_ref, v_ref, g_ref, gn_ref, kc_ref, vc_ref, o_ref, u_ref, s_ref = refs
    else:
        lg_ref, q_ref, k_ref, v_ref, g_ref, gn_ref, o_ref, u_ref, s_ref = refs
    c = chunk
    dk, dv = RET_DK, RET_DV
    ls = _log_sigmoid(lg_ref[...])
    lf = ls[0, 0:1, :]
    lb = ls[1, 0:1, :]
    lf_v = jnp.concatenate([lf] * (dv // V7X_LANES), axis=1)
    lb_v = jnp.concatenate([lb] * (dv // V7X_LANES), axis=1)
    lf_c = jnp.concatenate([lf] * (c // V7X_LANES), axis=1)
    lb_c = jnp.concatenate([lb] * (c // V7X_LANES), axis=1)
    idx = lax.broadcasted_iota(jnp.int32, (c, V7X_LANES), 0).astype(F32)

    def upd(i, carry):
        r0 = pl.multiple_of(i * c, c)
        u_ref[i] = _kv_update(k_ref[pl.ds(r0, c), :], v_ref[pl.ds(r0, c), :], lf, lb, idx)
        return carry
    lax.fori_loop(0, n_chunks, upd, 0, unroll=min(RET_UNROLL, n_chunks))

    if with_prefix:
        n_ctx = kc_ref.shape[0]
        idx_c = lax.broadcasted_iota(jnp.int32, (n_ctx, V7X_LANES), 0).astype(F32)
        u0 = _kv_update(kc_ref[...], vc_ref[...], lf, lb, idx_c)
        sf0, sb0 = u0[:dk], u0[dk:]
    else:
        sf0 = jnp.zeros((dk, dv), F32)
        sb0 = sf0
    gf_c = jnp.exp(lf_v * float(c))
    gb_c = jnp.exp(lb_v * float(c))

    def scan_f(i, sf):
        s_ref[i, :dk, :] = sf.astype(BF16)
        return sf * gf_c + u_ref[i, :dk, :]
    lax.fori_loop(0, n_chunks, scan_f, sf0)

    def scan_b(t, sb):
        i = n_chunks - 1 - t
        s_ref[i, dk:, :] = sb.astype(BF16)
        return sb * gb_c + u_ref[i, dk:, :]
    lax.fori_loop(0, n_chunks, scan_b, sb0)

    pr = (lax.broadcasted_iota(jnp.int32, (c, c), 0)
          - lax.broadcasted_iota(jnp.int32, (c, c), 1)).astype(F32)
    dmat = jnp.where(pr >= 0.0, jnp.exp(lf_c * jnp.maximum(pr, 0.0)),
                     jnp.exp(lb_c * jnp.maximum(-pr, 0.0)))
    qdf = jnp.exp(lf * (idx + 1.0)).astype(BF16)
    qdb = jnp.exp(lb * (float(c) - idx)).astype(BF16)
    gn = gn_ref[...].astype(BF16)
    gn_w, gn_b = gn[0:1], gn[1:2]

    def out(i, carry):
        r0 = pl.multiple_of(i * c, c)
        q = q_ref[pl.ds(r0, c), :]
        k = k_ref[pl.ds(r0, c), :]
        v = v_ref[pl.ds(r0, c), :]
        inner = (_dot_nt(q, k) * dmat).astype(BF16)
        qq = jnp.concatenate([q * qdf, q * qdb], axis=1)
        o = _dot(inner, v) + _dot(qq, s_ref[i])
        o_ref[pl.ds(r0, c), :] = (_ln_rows(o).astype(BF16) * gn_w + gn_b) * g_ref[pl.ds(r0, c), :]
        return carry
    lax.fori_loop(0, n_chunks, out, 0, unroll=min(RET_UNROLL, n_chunks))


def _retention(lg, q, k, v, g, gn, kc=None, vc=None, *, layer):
    b, n, _ = q.shape
    c = min(RET_CHUNK, n)
    nc = n // c
    with_prefix = kc is not None
    kern = functools.partial(_ret_kernel, n_chunks=nc, chunk=c, with_prefix=with_prefix)
    in_specs = [pl.BlockSpec((None, None, 2, 8, V7X_LANES), lambda bi, hi: (layer, hi, 0, 0, 0)),
                pl.BlockSpec((None, n, RET_DK), lambda bi, hi: (bi, 0, hi)),
                pl.BlockSpec((None, n, RET_DK), lambda bi, hi: (bi, 0, hi)),
                pl.BlockSpec((None, n, RET_DV), lambda bi, hi: (bi, 0, hi)),
                pl.BlockSpec((None, n, RET_DV), lambda bi, hi: (bi, 0, hi)),
                pl.BlockSpec((None, None, 2, RET_DV), lambda bi, hi: (layer, hi, 0, 0))]
    args = [lg, q, k, v, g, gn]
    if with_prefix:
        n_ctx = kc.shape[1]
        in_specs += [pl.BlockSpec((None, n_ctx, RET_DK), lambda bi, hi: (bi, 0, hi)),
                     pl.BlockSpec((None, n_ctx, RET_DV), lambda bi, hi: (bi, 0, hi))]
        args += [kc, vc]
    return pl.pallas_call(
        kern,
        grid=(b, RET_HEADS),
        in_specs=in_specs,
        out_specs=pl.BlockSpec((None, n, RET_DV), lambda bi, hi: (bi, 0, hi)),
        out_shape=jax.ShapeDtypeStruct((b, n, RET_HEADS * RET_DV), BF16),
        scratch_shapes=[pltpu.VMEM((nc, 2 * RET_DK, RET_DV), F32),
                        pltpu.VMEM((nc, 2 * RET_DK, RET_DV), BF16)],
        compiler_params=_cparams(("parallel", "parallel"), 56),
        name="retention",
    )(*args)


def _softmax_pv(s_parts, v_parts):
    m = None
    for s in s_parts:
        mi = jnp.max(s, axis=-1, keepdims=True)
        m = mi if m is None else jnp.maximum(m, mi)
    l = None
    o = None
    for s, v in zip(s_parts, v_parts):
        p = jnp.exp2(s - m)
        li = jnp.sum(p, axis=-1, keepdims=True)
        oi = _dot(p.astype(BF16), v)
        l = li if l is None else l + li
        o = oi if o is None else o + oi
    return o / l


def _na_kernel(q_ref, k_ref, v_ref, kc_ref, vc_ref, bias_ref, o_ref, s_ref, *, n_groups, key_row0_max):
    gq = NA_GROUP_ROWS * GRID_W
    nk = NA_KEY_ROWS * GRID_W
    lane = lax.broadcasted_iota(jnp.int32, (gq, V7X_LANES), 1)
    head0 = lane < NA_DH
    kc = kc_ref[...]
    vc = vc_ref[...]

    def key_start(g):
        kb = jnp.clip(g * NA_GROUP_ROWS - NA_KH // 2, 0, key_row0_max)
        return pl.multiple_of(kb * GRID_W, GRID_W)

    def scores(g, slot):
        q0 = pl.multiple_of(g * gq, gq)
        cfg = jnp.where(g == 0, 0, jnp.where(g == n_groups - 1, 2, 1))
        q = q_ref[pl.ds(q0, gq), :]
        kl = k_ref[pl.ds(key_start(g), nk), :]
        zero = jnp.zeros_like(q)
        for hh in range(2):
            qh = jnp.where(head0, q, zero) if hh == 0 else jnp.where(head0, zero, q)
            s_ref[slot, hh, :, :nk] = _dot_nt(qh, kl) + bias_ref[cfg, hh]
            s_ref[slot, hh, :, nk:] = _dot_nt(qh, kc)

    def finish(g, slot):
        q0 = pl.multiple_of(g * gq, gq)
        vl = v_ref[pl.ds(key_start(g), nk), :]
        outs = [_softmax_pv([s_ref[slot, hh, :, :nk], s_ref[slot, hh, :, nk:]], [vl, vc])
                for hh in range(2)]
        o_ref[pl.ds(q0, gq), :] = jnp.where(head0, outs[0], outs[1]).astype(BF16)

    scores(0, 0)

    def pair(j, carry):
        g = 2 * j
        scores(g + 1, 1)
        finish(g, 0)
        scores(jnp.minimum(g + 2, n_groups - 1), 0)
        finish(g + 1, 1)
        return carry
    lax.fori_loop(0, n_groups // 2, pair, 0, unroll=min(4, n_groups // 2))


def _na_bias_tables(rpb):
    h = rpb.shape[0]
    w = GRID_W
    cols = np.arange(w)
    cs = np.clip(cols - NA_KW // 2, 0, w - NA_KW)
    kcol = np.arange(w)[None, :]
    colmask = (kcol >= cs[:, None]) & (kcol < cs[:, None] + NA_KW)
    pad = w - 1
    off = pad + NA_KW - 1
    period = off + w + 1
    rpb_p = jnp.pad(rpb, ((0, 0), (0, 0), (pad, period - pad - rpb.shape[2])))
    skew = jnp.tile(rpb_p, (1, 1, w))[:, :, :w * (period - 1)].reshape(h, -1, w, period - 1)
    toep = skew[:, :, :, off:off + w]
    toep = jnp.where(colmask[None, None], toep * LOG2_E, NEG_BIG)
    n_dr = toep.shape[1]
    toep = jnp.concatenate([toep, jnp.full((h, 1, w, w), NEG_BIG, F32)], axis=1)
    dr = np.full((3, NA_GROUP_ROWS, NA_KEY_ROWS), n_dr, np.int32)
    for cfg in range(3):
        rel = NA_GROUP_ROWS * cfg
        for rr in range(NA_GROUP_ROWS):
            lo = (0, rr, NA_KEY_ROWS - NA_KH)[cfg]
            for i in range(lo, lo + NA_KH):
                dr[cfg, rr, i] = i - rel - rr + NA_KH - 1
    pairs = dr.reshape(3, NA_GROUP_ROWS, NA_KEY_ROWS // 2, 2)
    uniq = sorted({(int(a), int(c)) for a, c in pairs.reshape(-1, 2)})
    slot_of = {p: u for u, p in enumerate(uniq)}
    piece = np.array([[[slot_of[(int(a), int(c))] for a, c in row] for row in cfg] for cfg in pairs])
    left = jnp.take(toep, jnp.asarray([p[0] for p in uniq]), axis=1)
    right = jnp.take(toep, jnp.asarray([p[1] for p in uniq]), axis=1)
    pieces = jnp.concatenate([left, right], axis=-1)
    n_u = len(uniq)

    def assemble(p_ref, o_ref):
        for cfg in range(3):
            for rr in range(NA_GROUP_ROWS):
                for pc in range(NA_KEY_ROWS // 2):
                    o_ref[cfg, rr * w:(rr + 1) * w, pc * 2 * w:(pc + 1) * 2 * w] = (
                        p_ref[int(piece[cfg, rr, pc])])

    return pl.pallas_call(
        assemble,
        grid=(h,),
        in_specs=[pl.BlockSpec((None, n_u, w, 2 * w), lambda hi: (hi, 0, 0, 0))],
        out_specs=pl.BlockSpec((3, None, NA_GROUP_ROWS * w, NA_KEY_ROWS * w),
                               lambda hi: (0, hi, 0, 0)),
        out_shape=jax.ShapeDtypeStruct((3, h, NA_GROUP_ROWS * w, NA_KEY_ROWS * w), F32),
        compiler_params=_cparams(("parallel",), 32),
        name="na_bias_table",
    )(pieces)


def _neighbourhood_attention(nq, nk, nv, kc, vc, bias):
    b, n, hw = nq.shape
    rows = n // GRID_W
    n_ctx = kc.shape[1]
    n_groups = rows // NA_GROUP_ROWS
    pairs = NA_HEADS // 2
    gq = NA_GROUP_ROWS * GRID_W
    nkeys = NA_KEY_ROWS * GRID_W
    kern = functools.partial(_na_kernel, n_groups=n_groups, key_row0_max=rows - NA_KEY_ROWS)
    tok = pl.BlockSpec((None, n, V7X_LANES), lambda bi, pi: (bi, 0, pi))
    ctx = pl.BlockSpec((None, n_ctx, V7X_LANES), lambda bi, pi: (bi, 0, pi))
    return pl.pallas_call(
        kern,
        grid=(b, pairs),
        in_specs=[tok, tok, tok, ctx, ctx,
                  pl.BlockSpec((3, 2, gq, nkeys), lambda bi, pi: (0, pi, 0, 0))],
        out_specs=tok,
        out_shape=jax.ShapeDtypeStruct((b, n, hw), BF16),
        scratch_shapes=[pltpu.VMEM((2, 2, gq, nkeys + n_ctx), F32)],
        compiler_params=_cparams(("parallel", "parallel"), 48),
        name="neighbourhood_attention",
    )(nq, nk, nv, kc, vc, bias)


def _ctx_attn_kernel(q_ref, k_ref, v_ref, o_ref):
    q = q_ref[...]
    k = k_ref[...]
    v = v_ref[...]
    lane = lax.broadcasted_iota(jnp.int32, q.shape, 1)
    head0 = lane < NA_DH
    zero = jnp.zeros_like(q)
    outs = []
    for hh in range(2):
        qh = jnp.where(head0, q, zero) if hh == 0 else jnp.where(head0, zero, q)
        outs.append(_softmax_pv([_dot_nt(qh, k)], [v]))
    o_ref[...] = jnp.where(head0, outs[0], outs[1]).astype(BF16)


def _context_attention(q, k, v):
    b, n_ctx, hw = q.shape
    blk = pl.BlockSpec((None, n_ctx, V7X_LANES), lambda bi, pi: (bi, 0, pi))
    return pl.pallas_call(
        _ctx_attn_kernel,
        grid=(b, NA_HEADS // 2),
        in_specs=[blk, blk, blk],
        out_specs=blk,
        out_shape=jax.ShapeDtypeStruct((b, n_ctx, hw), BF16),
        compiler_params=_cparams(("parallel", "parallel"), 32),
        name="context_attention",
    )(q, k, v)


def _dft_rows_kernel(zre_ref, zim_ref, w_ref, are_ref, aim_ref):
    r = zre_ref.shape[0]
    z = jnp.concatenate([zre_ref[...].astype(BF16), zim_ref[...].astype(BF16)], axis=0)
    a = _dot(w_ref[...], z)
    are_ref[...] = a[:r].astype(BF16)
    aim_ref[...] = a[r:].astype(BF16)


def _dft_rows(zre, zim, w2):
    b, r, wc = zre.shape
    cb = min(4096, wc)
    blk = pl.BlockSpec((None, r, cb), lambda bi, ci: (bi, 0, ci))
    return pl.pallas_call(
        _dft_rows_kernel,
        grid=(b, wc // cb),
        in_specs=[blk, blk, _resident((2 * r, 2 * r))],
        out_specs=[blk, blk],
        out_shape=[jax.ShapeDtypeStruct((b, r, wc), BF16)] * 2,
        compiler_params=_cparams(("parallel", "parallel"), 40),
        name="dft_rows",
    )(zre, zim, w2)


def _dft_cols_kernel(are_ref, aim_ref, twc_ref, tws_ref, km_ref, wo_ref, o_ref, *, scale, ch, jb):
    w = are_ref.shape[1] // ch
    reps = ch // V7X_LANES
    slabs_re = [are_ref[:, c * ch:(c + 1) * ch].astype(F32) for c in range(w)]
    slabs_im = [aim_ref[:, c * ch:(c + 1) * ch].astype(F32) for c in range(w)]
    for hf in range(are_ref.shape[0] // jb):
        rs = slice(hf * jb, (hf + 1) * jb)
        ts_rows = slice(hf * jb * w, (hf + 1) * jb * w)
        ar = jnp.concatenate([s[rs] for s in slabs_re], axis=0)
        ai = jnp.concatenate([s[rs] for s in slabs_im], axis=0)
        tc = jnp.concatenate([twc_ref[ts_rows, :]] * reps, axis=1)
        ts = jnp.concatenate([tws_ref[ts_rows, :]] * reps, axis=1)
        a = jnp.concatenate([(ar * tc - ai * ts).astype(BF16), (ar * ts + ai * tc).astype(BF16)],
                            axis=0)
        y = _dot(km_ref[...], a) * scale
        out = _dot(y.astype(BF16), wo_ref[...])
        o_ref[:, rs, :] = out.reshape(w, jb, out.shape[1])


def _dft_cols(a_re, a_im, twc, tws, kmat, w_out, *, layer, scale):
    b, r, wc = a_re.shape
    w = GRID_W
    ch = wc // w
    d = w_out.shape[2]
    jb = FOU_K1_BLOCK
    ab = FOU_A_ROWS
    blk = pl.BlockSpec((None, ab, wc), lambda bi, ji: (bi, ji, 0))
    tw = pl.BlockSpec((ab * w, V7X_LANES), lambda bi, ji: (ji, 0))
    kern = functools.partial(_dft_cols_kernel, scale=scale, ch=ch, jb=jb)
    return pl.pallas_call(
        kern,
        grid=(b, r // ab),
        in_specs=[blk, blk, tw, tw, _resident(kmat.shape), _resident((ch, d), layer)],
        out_specs=pl.BlockSpec((None, w, ab, d), lambda bi, ji: (bi, 0, ji, 0)),
        out_shape=jax.ShapeDtypeStruct((b, w, r, d), F32),
        compiler_params=_cparams(("parallel", "parallel"), 40),
        name="dft_cols",
    )(a_re, a_im, twc, tws, kmat, w_out)


def _ctx_fourier_kernel(zre_ref, zim_ref, cm_ref, wo_ref, o_ref, *, scale):
    z = jnp.concatenate([zre_ref[...], zim_ref[...]], axis=0)
    y = _dot(cm_ref[...], z) * scale
    o_ref[...] = _dot(y.astype(BF16), wo_ref[...])


def _ctx_fourier(zre, zim, cmat, w_out, *, layer, scale):
    b, n_ctx, ch = zre.shape
    d = w_out.shape[2]
    blk = pl.BlockSpec((None, n_ctx, ch), lambda bi: (bi, 0, 0))
    kern = functools.partial(_ctx_fourier_kernel, scale=scale)
    return pl.pallas_call(
        kern,
        grid=(b,),
        in_specs=[blk, blk, _resident(cmat.shape), _resident((ch, d), layer)],
        out_specs=pl.BlockSpec((None, n_ctx, d), lambda bi: (bi, 0, 0)),
        out_shape=jax.ShapeDtypeStruct((b, n_ctx, d), F32),
        compiler_params=_cparams(("parallel",), 32),
        name="ctx_fourier",
    )(zre, zim, cmat, w_out)


def _merge_kernel(x_ref, mods_ref, ret_ref, na_ref, fou_ref, gt_ref, wr_ref, wn_ref, wo_ref, ln_ref,
                  o_ref, *, alpha):
    d = x_ref.shape[1]
    m = mods_ref[...]
    ln = ln_ref[...]
    for rows in _row_splits(x_ref.shape[0]):
        y_ret = _dot(ret_ref[rows, :], wr_ref[...])
        y_na = _dot(na_ref[rows, :], wn_ref[...])
        mix = (gt_ref[rows, 0:d].astype(F32) * y_ret + gt_ref[rows, d:2 * d].astype(F32) * y_na
               + gt_ref[rows, 2 * d:3 * d].astype(F32) * fou_ref[rows, :])
        y = _dot(mix.astype(BF16), wo_ref[...])
        z = alpha * x_ref[rows, :] + m[5:6] * y
        o_ref[rows, :] = _ln_rows(z) * ln[1:2] + ln[4:5]


def _merge(x2, mods, mods_row, ret, na, fou, gt, wr, wn, wo, ln, *, layer, tm, alpha):
    t, d = x2.shape
    kern = functools.partial(_merge_kernel, alpha=alpha)

    def rows(wd):
        return pl.BlockSpec((tm, wd), lambda i: (i, 0))
    return pl.pallas_call(
        kern,
        grid=(t // tm,),
        in_specs=[rows(d), pl.BlockSpec((None, None, 9, d), lambda i: (layer, mods_row(i), 0, 0)),
                  rows(ret.shape[1]), rows(na.shape[1]), rows(d), rows(gt.shape[1]),
                  _resident(wr.shape[1:], layer), _resident(wn.shape[1:], layer),
                  _resident(wo.shape[1:], layer), _resident((6, d), layer)],
        out_specs=rows(d),
        out_shape=jax.ShapeDtypeStruct((t, d), F32),
        compiler_params=_cparams(("parallel",), 60 if tm > TOKEN_TILE else 48),
        name="merge",
    )(x2, mods, ret, na, fou, gt, wr, wn, wo, ln)


def _rope_tables(n):
    half = RET_DK // 4
    t = np.arange(n)
    row, col = t // GRID_W, t % GRID_W
    freqs = ROPE_BASE ** (-np.arange(half, dtype=np.float64) / half)
    ang_r = row.astype(np.float64)[:, None] * freqs[None, :]
    ang_c = col.astype(np.float64)[:, None] * freqs[None, :]
    cos = np.concatenate([np.cos(ang_r)] * 2 + [np.cos(ang_c)] * 2, axis=1)
    sin = np.concatenate([-np.sin(ang_r), np.sin(ang_r), -np.sin(ang_c), np.sin(ang_c)], axis=1)
    return jnp.asarray(cos, F32), jnp.asarray(sin, F32)


def _dft_tables(n):
    w = GRID_W
    r = n // w
    jb = FOU_K1_BLOCK
    two_pi = 2.0 * np.pi
    ch = np.arange(FOU_DG)
    ang = two_pi * ((ch[:, None] * ch[None, :]) % FOU_DG) / FOU_DG
    cs = np.concatenate([np.cos(ang), np.sin(ang)], axis=1)
    k1 = np.arange(r)
    ang = two_pi * ((k1[:, None] * k1[None, :]) % r) / r
    w2 = np.block([[np.cos(ang), -np.sin(ang)], [np.sin(ang), np.cos(ang)]])
    c = np.arange(w)
    k1_blocks = k1.reshape(r // jb, 1, jb)
    ang = two_pi * (k1_blocks * c[None, :, None]).reshape(-1) / n
    twc = np.repeat(np.cos(ang)[:, None], V7X_LANES, axis=1)
    tws = np.repeat(np.sin(ang)[:, None], V7X_LANES, axis=1)
    ang = two_pi * ((c[:, None] * c[None, :]) % w) / w
    eye = np.eye(jb)
    kc = np.einsum("kc,ji->kjci", np.cos(ang), eye).reshape(w * jb, w * jb)
    ks = np.einsum("kc,ji->kjci", np.sin(ang), eye).reshape(w * jb, w * jb)
    kmat = np.concatenate([kc, -ks], axis=1)
    as_bf16 = lambda a: jnp.asarray(a, F32).astype(BF16)
    return (as_bf16(cs), as_bf16(w2), jnp.asarray(twc, F32), jnp.asarray(tws, F32), as_bf16(kmat))


def _ctx_dft_matrix(n_ctx):
    m = np.arange(n_ctx)
    ang = 2.0 * np.pi * ((m[:, None] * m[None, :]) % n_ctx) / n_ctx
    return jnp.asarray(np.concatenate([np.cos(ang), -np.sin(ang)], axis=1), F32).astype(BF16)


def kernel(x, c, ctx, c_ctx, ada_w, ada_b, ln_w, ln_b, ffa_w1, ffa_w3, ffa_w2, mix_w_in,
           ret_decay_logit, ret_gn_w, ret_gn_b, ret_w_out, na_rpb, na_w_out, fou_w_out, mix_w_o,
           ffb_w1, ffb_w3, ffb_w2):
    b, n, d = x.shape
    n_ctx = ctx.shape[1]
    depth = ada_w.shape[0]
    rows = n // GRID_W
    assert n % TOKEN_TILE == 0 and rows % (2 * NA_GROUP_ROWS) == 0 and rows >= NA_KEY_ROWS
    assert rows % FOU_A_ROWS == 0 and b + 1 <= 8
    alpha = (2 * depth) ** 0.25
    tm = TOKEN_TILE
    tm_ffn = FFN_TILE if n % FFN_TILE == 0 else TOKEN_TILE
    tiles_per_seq = n // tm
    lat_row = lambda i: i // tiles_per_seq
    ffn_row = lambda i: i // (n // tm_ffn)
    ctx_row = lambda i: b
    lat_rope = lambda i: i % tiles_per_seq
    ctx_rope = lambda i: 0
    t_ctx = b * n_ctx
    tc = TOKEN_TILE if t_ctx % TOKEN_TILE == 0 else n_ctx
    tc_ffn = FFN_TILE if t_ctx % FFN_TILE == 0 else tc

    cos_x, sin_x = _rope_tables(n)
    cos_c = jnp.ones((tc, V7X_LANES), F32)
    sin_c = jnp.zeros((tc, V7X_LANES), F32)
    cs, w2, twc, tws, kmat = _dft_tables(n)
    cmat_ctx = _ctx_dft_matrix(n_ctx)
    fou_scale_x = 1.0 / math.sqrt(n * FOU_DG)
    fou_scale_c = 1.0 / math.sqrt(n_ctx * FOU_DG)

    cvec = jnp.concatenate([c, c_ctx[None, :], jnp.zeros((8 - b - 1, d), F32)], axis=0)
    mods_all = _ada_mods(cvec, ada_w, ada_b).reshape(depth, 8, 9, d)
    ln_all = jnp.concatenate([ln_w, ln_b], axis=1)
    to_bf16 = lambda a: a.astype(BF16)
    ffa = [to_bf16(a) for a in (ffa_w1, ffa_w3, ffa_w2)]
    ffb = [to_bf16(a) for a in (ffb_w1, ffb_w3, ffb_w2)]
    w_in, wr, wn, wf, wo = [to_bf16(a) for a in (mix_w_in, ret_w_out, na_w_out, fou_w_out, mix_w_o)]
    lg_all = jnp.broadcast_to(
        jnp.transpose(ret_decay_logit, (0, 2, 1))[:, :, :, None, None],
        (depth, RET_HEADS, 2, 8, V7X_LANES)).astype(F32)
    gn_all = jnp.stack([ret_gn_w.reshape(depth, RET_HEADS, RET_DV),
                        ret_gn_b.reshape(depth, RET_HEADS, RET_DV)], axis=2)

    xs = x.reshape(b * n, d)
    cx = ctx.reshape(b * n_ctx, d)
    mods, ln = mods_all, ln_all
    for l in range(depth):
        last = l == depth - 1
        xs = _ffn(xs, mods, ffn_row, *ffa, ln, layer=l, tm=tm_ffn, mod_base=0, ln_row=0, alpha=alpha)
        cx = _ffn(cx, mods, ctx_row, *ffa, ln, layer=l, tm=tc_ffn, mod_base=0, ln_row=0, alpha=alpha)

        px = _proj(xs, mods, lat_row, w_in, cos_x, sin_x, lat_rope, cs, layer=l, tm=tm,
                   grid_rows_out=True)
        pc = _proj(cx, mods, ctx_row, w_in, cos_c, sin_c, ctx_rope, cs, layer=l, tm=tc,
                   grid_rows_out=False)
        q_x, k_x, v_x, g_x, nq_x, nk_x, nv_x = [a.reshape(b, n, -1) for a in px[:7]]
        gt_x = px[9]
        q_c, k_c, v_c, g_c, nq_c, nk_c, nv_c, zre_c, zim_c, gt_c = [a.reshape(b, n_ctx, -1) for a in pc]

        ret_x = _retention(lg_all, q_x, k_x, v_x, g_x, gn_all, k_c, v_c, layer=l)
        bias = _na_bias_tables(na_rpb[l])
        na_x = _neighbourhood_attention(nq_x, nk_x, nv_x, nk_c, nv_c, bias)
        a_re, a_im = _dft_rows(px[7].reshape(b, rows, -1), px[8].reshape(b, rows, -1), w2)
        fou_x = _dft_cols(a_re, a_im, twc, tws, kmat, wf, layer=l, scale=fou_scale_x)
        xs = _merge(xs, mods, ffn_row, ret_x.reshape(b * n, -1), na_x.reshape(b * n, -1),
                    fou_x.reshape(b * n, d), gt_x, wr, wn, wo, ln,
                    layer=l, tm=tm_ffn, alpha=alpha)
        xs = _ffn(xs, mods, ffn_row, *ffb, ln, layer=l, tm=tm_ffn, mod_base=6, ln_row=2, alpha=alpha)
        if not last:
            ret_c = _retention(lg_all, q_c, k_c, v_c, g_c, gn_all, layer=l)
            na_c = _context_attention(nq_c, nk_c, nv_c)
            fou_c = _ctx_fourier(zre_c, zim_c, cmat_ctx, wf, layer=l, scale=fou_scale_c)
            cx = _merge(cx, mods, ctx_row, ret_c.reshape(b * n_ctx, -1), na_c.reshape(b * n_ctx, -1),
                        fou_c.reshape(b * n_ctx, d), gt_c.reshape(b * n_ctx, -1), wr, wn, wo, ln,
                        layer=l, tm=tc_ffn, alpha=alpha)
            cx = _ffn(cx, mods, ctx_row, *ffb, ln, layer=l, tm=tc_ffn, mod_base=6, ln_row=2,
                      alpha=alpha)
    return xs.reshape(b, n, d)
```
